```python
import jax, jax.numpy as jnp
from jax import lax
import numpy as np

D_MODEL = 2048
BATCH = 32
SEQ = 256
DEPTH = 4
DEC_BATCH = 8
DEC_SEQ = 4096
PAST_LEN = 256

GRID_W = 64
ATT_HEADS = 16
KV_HEADS = 4
HEAD_DIM = 128
Q_GROUPS = ATT_HEADS // KV_HEADS
Q_BLOCK = 128
ROPE_THETA = 10000.0
AXIS_PAIRS = HEAD_DIM // 4
GLA_HEADS = 4
GLA_DK = D_MODEL // 8
GLA_DV = D_MODEL // 4
GLA_KEY = GLA_HEADS * GLA_DK
GLA_VAL = GLA_HEADS * GLA_DV
GK_RANK = 16
GK_NORMALIZER = 16.0
GLA_CHUNK = 64
FF_DENSE = ((8 * D_MODEL // 3 + 255) // 256) * 256
N_EXPERTS = 8
TOP_K = 2
FF_EXPERT = 7 * D_MODEL // 2
MOE_BLOCK = 128
N_DENSE = (DEPTH + 1) // 2
N_MOE = DEPTH // 2
EPS = 1e-6

_IN_SIZES = (ATT_HEADS * HEAD_DIM, KV_HEADS * HEAD_DIM, KV_HEADS * HEAD_DIM,
             GLA_KEY, GLA_KEY, GLA_VAL, GLA_VAL, GK_RANK, GK_RANK, D_MODEL, D_MODEL)
IN_SPLITS = tuple(int(s) for s in np.cumsum(_IN_SIZES)[:-1])
IN_COLS = int(sum(_IN_SIZES))

kernel_name = "hybrid_gqa_gla_prefix_dit_step"


def rms_norm(x, g):
    xf = x.astype(jnp.float32)
    y = xf * lax.rsqrt(jnp.mean(xf * xf, axis=-1, keepdims=True) + EPS)
    return (y * g.astype(jnp.float32)).astype(x.dtype)


def modulation(cond, w, b):
    m = (jax.nn.silu(cond) @ w + b)[:, None, :]
    return jnp.split(m, 6, axis=-1)


def axial_rope(n_tokens):
    rows = n_tokens // GRID_W
    row = jnp.repeat(jnp.arange(rows, dtype=jnp.float32), GRID_W)
    col = jnp.tile(jnp.arange(GRID_W, dtype=jnp.float32), rows)
    freqs = ROPE_THETA ** (-jnp.arange(AXIS_PAIRS, dtype=jnp.float32) / AXIS_PAIRS)
    ang = jnp.concatenate([row[:, None] * freqs, col[:, None] * freqs], axis=-1)
    return jnp.cos(ang)[:, None, :], jnp.sin(ang)[:, None, :]


def apply_rope(x, cos, sin):
    xf = x.astype(jnp.float32)
    x1, x2 = jnp.split(xf, 2, axis=-1)
    return jnp.concatenate([x1 * cos - x2 * sin, x2 * cos + x1 * sin], axis=-1).astype(x.dtype)


def block_attention(q, k, v):
    B, Tq = q.shape[:2]
    nb = Tq // Q_BLOCK
    qb = q.reshape(B, nb, Q_BLOCK, KV_HEADS, Q_GROUPS, HEAD_DIM).transpose(1, 0, 2, 3, 4, 5)
    scale = HEAD_DIM ** -0.5

    def one_block(qblk):
        s = jnp.einsum('bqkgd,bskd->bkgqs', qblk, k, preferred_element_type=jnp.float32) * scale
        p = jax.nn.softmax(s, axis=-1).astype(v.dtype)
        return jnp.einsum('bkgqs,bskd->bqkgd', p, v)

    o = lax.map(one_block, qb)
    return o.transpose(1, 0, 2, 3, 4, 5).reshape(B, Tq, ATT_HEADS * HEAD_DIM)


def gla_scan(q, k, v, g, s0):
    B, T = q.shape[:2]
    n = T // GLA_CHUNK

    def chunks(a):
        return a.astype(jnp.float32).reshape(B, n, GLA_CHUNK, GLA_HEADS, a.shape[-1]).transpose(1, 0, 3, 2, 4)

    causal = jnp.tril(jnp.ones((GLA_CHUNK, GLA_CHUNK), dtype=bool))
    mid = GLA_CHUNK // 2

    def step(S, inp):
        qc, kc, vc, gc = inp
        b = jnp.cumsum(gc, axis=2)
        b_ref = b[:, :, mid:mid + 1]
        b_last = b[:, :, -1:]
        o = jnp.einsum('bhcd,bhde->bhce', qc * jnp.exp(b), S)
        a = jnp.einsum('bhid,bhjd->bhij', qc * jnp.exp(b - b_ref), kc * jnp.exp(b_ref - b))
        o = o + jnp.einsum('bhij,bhje->bhie', jnp.where(causal, a, 0.0), vc)
        S = jnp.exp(b_last[:, :, 0, :, None]) * S + jnp.einsum('bhcd,bhce->bhde', kc * jnp.exp(b_last - b), vc)
        return S, o

    S, o = lax.scan(step, s0.astype(jnp.float32), (chunks(q), chunks(k), chunks(v), chunks(g)))
    o = o.transpose(1, 0, 3, 2, 4).reshape(B, T, GLA_HEADS, v.shape[-1])
    return o, S


def token_mixing(h, lp, ctx, rope):
    B, T, _ = h.shape
    q, k, v, gq, gk, gv, gg, lr_f, lr_b, gate_a, gate_b = jnp.split(h @ lp['w_in'], IN_SPLITS, axis=-1)

    q = rms_norm(q.reshape(B, T, ATT_HEADS, HEAD_DIM), lp['q_norm'])
    k = rms_norm(k.reshape(B, T, KV_HEADS, HEAD_DIM), lp['k_norm'])
    v = v.reshape(B, T, KV_HEADS, HEAD_DIM)
    if rope is not None:
        q = apply_rope(q, *rope)
        k_att = apply_rope(k, *rope)
    else:
        k_att = k
    if ctx is None:
        keys, vals = k_att, v
    else:
        keys = jnp.concatenate([ctx[0].astype(k_att.dtype), k_att], axis=1)
        vals = jnp.concatenate([ctx[1].astype(v.dtype), v], axis=1)
    o_att = block_attention(q, keys, vals)

    gq = gq.reshape(B, T, GLA_HEADS, GLA_DK) * (GLA_DK ** -0.5)
    gk = gk.reshape(B, T, GLA_HEADS, GLA_DK)
    gv = gv.reshape(B, T, GLA_HEADS, GLA_DV)
    dec_f = (jax.nn.log_sigmoid((lr_f @ lp['w_gk'][0] + lp['b_gk'][0]).astype(jnp.float32)) / GK_NORMALIZER
             ).reshape(B, T, GLA_HEADS, GLA_DK)
    dec_b = (jax.nn.log_sigmoid((lr_b @ lp['w_gk'][1] + lp['b_gk'][1]).astype(jnp.float32)) / GK_NORMALIZER
             ).reshape(B, T, GLA_HEADS, GLA_DK)
    if ctx is None:
        s0f = jnp.zeros((B, GLA_HEADS, GLA_DK, GLA_DV), jnp.float32)
        s0b = s0f
    else:
        s0f, s0b = ctx[2], ctx[3]
    o_f, s_f = gla_scan(gq, gk, gv, dec_f, s0f)
    flip = lambda a: jnp.flip(a, axis=1)
    o_b, s_b = gla_scan(flip(gq), flip(gk), flip(gv), flip(dec_b), s0b)
    o_gla = (o_f + flip(o_b)).astype(h.dtype)
    o_gla = rms_norm(o_gla, lp['gla_norm']).reshape(B, T, GLA_VAL) * jax.nn.silu(gg)

    merged = jax.nn.sigmoid(gate_a) * o_att + jax.nn.sigmoid(gate_b) * o_gla
    return merged @ lp['w_out'], (k, v, s_f.astype(h.dtype), s_b.astype(h.dtype))


def swiglu(h, wg, wu, wd):
    return (jax.nn.silu(h @ wg) * (h @ wu)) @ wd


def moe_swiglu(h, w_router, wg, wu, wd):
    B, T, D = h.shape
    n_tok = B * T
    ht = h.reshape(n_tok, D)
    logits = (ht @ w_router).astype(jnp.float32)
    top_v, top_i = lax.top_k(logits, TOP_K)
    weights = jax.nn.softmax(top_v, axis=-1)
    n_assign = n_tok * TOP_K
    flat_e = top_i.reshape(n_assign)
    flat_w = weights.reshape(n_assign)
    flat_tok = jnp.arange(n_assign, dtype=jnp.int32) // TOP_K
    onehot = jax.nn.one_hot(flat_e, N_EXPERTS, dtype=jnp.int32)
    counts = jnp.sum(onehot, axis=0)
    rank = jnp.take_along_axis(jnp.cumsum(onehot, axis=0), flat_e[:, None], axis=1)[:, 0] - 1
    padded = ((counts + MOE_BLOCK - 1) // MOE_BLOCK) * MOE_BLOCK
    pad_end = jnp.cumsum(padded)
    pad_start = pad_end - padded
    dest = pad_start[flat_e] + rank
    n_blocks = (n_assign + MOE_BLOCK - 1) // MOE_BLOCK + N_EXPERTS
    rows = n_blocks * MOE_BLOCK
    buf_tok = jnp.zeros((rows,), jnp.int32).at[dest].set(flat_tok)
    buf_w = jnp.zeros((rows,), jnp.float32).at[dest].set(flat_w)
    block_expert = jnp.minimum(
        jnp.searchsorted(pad_end, jnp.arange(n_blocks, dtype=pad_end.dtype) * MOE_BLOCK, side='right'),
        N_EXPERTS - 1)

    def run(args):
        tok, wt, e = args
        y = swiglu(ht[tok], wg[e], wu[e], wd[e])
        return y * wt[:, None].astype(y.dtype)

    ys = lax.map(run, (buf_tok.reshape(n_blocks, MOE_BLOCK), buf_w.reshape(n_blocks, MOE_BLOCK), block_expert))
    out = jnp.zeros((n_tok, D), h.dtype).at[buf_tok].add(ys.reshape(rows, D).astype(h.dtype))
    return out.reshape(B, T, D)


def trunk_layer(x, cond, lp, use_moe, ctx, rope):
    sh1, sc1, g1, sh2, sc2, g2 = modulation(cond, lp['w_mod'], lp['b_mod'])
    h = rms_norm(x, lp['norm1']) * (1 + sc1) + sh1
    mix, side = token_mixing(h, lp, ctx, rope)
    x = x + g1 * mix
    h = rms_norm(x, lp['norm2']) * (1 + sc2) + sh2
    if use_moe:
        f = moe_swiglu(h, lp['w_router'], lp['w_e_gate'], lp['w_e_up'], lp['w_e_down'])
    else:
        f = swiglu(h, lp['w_ff_gate'], lp['w_ff_up'], lp['w_ff_down'])
    return x + g2 * f, side


def setup_inputs(seed: int = 0) -> dict:
    key = jax.random.key(seed)
    ks = jax.random.split(key, 32)
    f32 = jnp.float32
    nrm = lambda k, shape, s: jax.random.normal(k, shape, f32) * s
    sD = D_MODEL ** -0.5
    return {
        'x_prompt': nrm(ks[0], (BATCH, SEQ, D_MODEL), 1.0),
        'x_sample': nrm(ks[1], (DEC_BATCH, DEC_SEQ, D_MODEL), 1.0),
        'cache_k': nrm(ks[2], (DEC_BATCH, DEPTH, PAST_LEN, KV_HEADS, HEAD_DIM), 1.0),
        'cache_v': nrm(ks[3], (DEC_BATCH, DEPTH, PAST_LEN, KV_HEADS, HEAD_DIM), 1.0),
        'state_fwd': nrm(ks[4], (DEC_BATCH, DEPTH, GLA_HEADS, GLA_DK, GLA_DV), 1.0),
        'state_bwd': nrm(ks[5], (DEC_BATCH, DEPTH, GLA_HEADS, GLA_DK, GLA_DV), 1.0),
        'c': nrm(ks[6], (DEC_BATCH, D_MODEL), 1.0),
        'c_ctx': nrm(ks[7], (D_MODEL,), 1.0),
        'w_mod': nrm(ks[8], (DEPTH, D_MODEL, 6 * D_MODEL), 0.5 * sD),
        'b_mod': nrm(ks[9], (DEPTH, 6 * D_MODEL), 0.01),
        'norm1': 1.0 + nrm(ks[10], (DEPTH, D_MODEL), 0.01),
        'w_in': nrm(ks[11], (DEPTH, D_MODEL, IN_COLS), sD),
        'q_norm': 1.0 + nrm(ks[12], (DEPTH, HEAD_DIM), 0.01),
        'k_norm': 1.0 + nrm(ks[13], (DEPTH, HEAD_DIM), 0.01),
        'w_gk': nrm(ks[14], (DEPTH, 2, GK_RANK, GLA_KEY), GK_RANK ** -0.5),
        'b_gk': nrm(ks[15], (DEPTH, 2, GLA_KEY), 0.01),
        'gla_norm': 1.0 + nrm(ks[16], (DEPTH, GLA_DV), 0.01),
        'w_out': nrm(ks[17], (DEPTH, D_MODEL, D_MODEL), sD),
        'norm2': 1.0 + nrm(ks[18], (DEPTH, D_MODEL), 0.01),
        'w_ff_gate': nrm(ks[19], (N_DENSE, D_MODEL, FF_DENSE), sD),
        'w_ff_up': nrm(ks[20], (N_DENSE, D_MODEL, FF_DENSE), sD),
        'w_ff_down': nrm(ks[21], (N_DENSE, FF_DENSE, D_MODEL), FF_DENSE ** -0.5),
        'w_router': nrm(ks[22], (N_MOE, D_MODEL, N_EXPERTS), sD),
        'w_e_gate': nrm(ks[23], (N_MOE, N_EXPERTS, D_MODEL, FF_EXPERT), sD),
        'w_e_up': nrm(ks[24], (N_MOE, N_EXPERTS, D_MODEL, FF_EXPERT), sD),
        'w_e_down': nrm(ks[25], (N_MOE, N_EXPERTS, FF_EXPERT, D_MODEL), FF_EXPERT ** -0.5),
        'final_norm': 1.0 + nrm(ks[26], (D_MODEL,), 0.01),
    }


def reference(x_prompt, x_sample, cache_k, cache_v, state_fwd, state_bwd, c, c_ctx,
              w_mod, b_mod, norm1, w_in, q_norm, k_norm, w_gk, b_gk, gla_norm, w_out, norm2,
              w_ff_gate, w_ff_up, w_ff_down, w_router, w_e_gate, w_e_up, w_e_down, final_norm):
    rope = axial_rope(x_sample.shape[1])
    xp, xs = x_prompt, x_sample
    cond_ctx = c_ctx[None, :]
    new_k, new_v, new_sf, new_sb = [], [], [], []
    for l in range(DEPTH):
        use_moe = (l % 2 == 1)
        lp = {'w_mod': w_mod[l], 'b_mod': b_mod[l], 'norm1': norm1[l], 'w_in': w_in[l],
              'q_norm': q_norm[l], 'k_norm': k_norm[l], 'w_gk': w_gk[l], 'b_gk': b_gk[l],
              'gla_norm': gla_norm[l], 'w_out': w_out[l], 'norm2': norm2[l]}
        if use_moe:
            i = l // 2
            lp.update({'w_router': w_router[i], 'w_e_gate': w_e_gate[i], 'w_e_up': w_e_up[i], 'w_e_down': w_e_down[i]})
        else:
            i = l // 2
            lp.update({'w_ff_gate': w_ff_gate[i], 'w_ff_up': w_ff_up[i], 'w_ff_down': w_ff_down[i]})
        xp, (k_l, v_l, sf_l, sb_l) = trunk_layer(xp, cond_ctx, lp, use_moe, None, None)
        new_k.append(k_l)
        new_v.append(v_l)
        new_sf.append(sf_l)
        new_sb.append(sb_l)
        ctx = (cache_k[:, l], cache_v[:, l], state_fwd[:, l], state_bwd[:, l])
        xs, _ = trunk_layer(xs, c, lp, use_moe, ctx, rope)
    y_prompt = rms_norm(xp, final_norm)
    y_sample = rms_norm(xs, final_norm)
    k_out = jnp.stack(new_k, axis=1)
    v_out = jnp.stack(new_v, axis=1)
    sf_out = jnp.stack(new_sf, axis=1)
    sb_out = jnp.stack(new_sb, axis=1)
    return (y_prompt, y_sample, k_out, v_out, sf_out, sb_out)
```

```python
import functools

import jax
import jax.numpy as jnp
import numpy as np
from jax import lax
from jax.experimental import pallas as pl
from jax.experimental.pallas import tpu as pltpu

D_MODEL = 2048
GRID_W = 64
ATT_HEADS = 16
KV_HEADS = 4
HEAD_DIM = 128
Q_GROUPS = ATT_HEADS // KV_HEADS
ROPE_THETA = 10000.0
AXIS_PAIRS = HEAD_DIM // 4
GLA_HEADS = 4
GLA_DK = D_MODEL // 8
GLA_DV = D_MODEL // 4
GLA_KEY = GLA_HEADS * GLA_DK
GLA_VAL = GLA_HEADS * GLA_DV
GK_RANK = 16
GK_NORMALIZER = 16.0
GLA_CHUNK = 64
FF_DENSE = ((8 * D_MODEL // 3 + 255) // 256) * 256
N_EXPERTS = 8
FF_EXPERT = 7 * D_MODEL // 2
EPS = 1e-6

LANES = 128
N_COND_PAD = 16
VMEM_LIMIT = 56 * 1024 * 1024

COL_Q = 0
COL_GV = COL_Q + ATT_HEADS * HEAD_DIM
COL_GG = COL_GV + GLA_VAL
COL_GA = COL_GG + GLA_VAL
COL_GB = COL_GA + D_MODEL
COL_GQ = COL_GB + D_MODEL
COL_GK = COL_GQ + GLA_KEY
COL_K = COL_GK + GLA_KEY
COL_V = COL_K + KV_HEADS * HEAD_DIM
MAIN_COLS = COL_V + KV_HEADS * HEAD_DIM
PROJ_TN = Q_GROUPS * HEAD_DIM
GLA_BLOCK = 4 * GLA_CHUNK

f32 = jnp.float32
bf16 = jnp.bfloat16
NT_DIMS = (((1,), (1,)), ((), ()))
TN_DIMS = (((0,), (0,)), ((), ()))


def _params(semantics):
    return pltpu.CompilerParams(dimension_semantics=semantics, vmem_limit_bytes=VMEM_LIMIT)


def _tile(n, pref):
    t = pref
    while n % t:
        t //= 2
    return t


def _cond_index(tile_rows, n_prompt, dec_seq):
    npt = n_prompt // tile_rows
    per = dec_seq // tile_rows
    return lambda i: jnp.where(i < npt, 0, 1 + (i - npt) // per)


def _silu(x):
    return x * jax.nn.sigmoid(x)


def _rms(x, gain):
    return x * lax.rsqrt(jnp.mean(x * x, axis=-1, keepdims=True) + EPS) * gain


def _mod_kernel(c_ref, w_ref, b_ref, o_ref):
    s = _silu(c_ref[...]).astype(bf16)
    o_ref[...] = jnp.dot(s, w_ref[...].astype(bf16), preferred_element_type=f32) + b_ref[...]


def _modulation(cond, w_mod, b_mod):
    depth = w_mod.shape[0]
    tn = 1024
    out = pl.pallas_call(
        _mod_kernel,
        grid=(depth, 6 * D_MODEL // tn),
        in_specs=[
            pl.BlockSpec((N_COND_PAD, D_MODEL), lambda l, n: (0, 0)),
            pl.BlockSpec((None, D_MODEL, tn), lambda l, n: (l, 0, n)),
            pl.BlockSpec((None, 1, tn), lambda l, n: (l, 0, n)),
        ],
        out_specs=pl.BlockSpec((None, N_COND_PAD, tn), lambda l, n: (l, 0, n)),
        out_shape=jax.ShapeDtypeStruct((depth, N_COND_PAD, 6 * D_MODEL), f32),
        compiler_params=_params(("arbitrary", "arbitrary")),
        name="modulation",
    )(cond, w_mod, b_mod.reshape(depth, 1, 6 * D_MODEL))
    return out.reshape(depth, N_COND_PAD, 6, D_MODEL)


def _in_proj_kernel(x_ref, mod_ref, n1_ref, w_ref, wlr_ref, qg_ref, kg_ref, cos_ref, sin_ref,
                    main_ref, lr_ref, kv_ref, h_ref):
    j = pl.program_id(1)
    jk = COL_K // PROJ_TN
    jv = COL_V // PROJ_TN

    @pl.when(j == 0)
    def _():
        y = _rms(x_ref[...], n1_ref[...])
        hb = (y * (1.0 + mod_ref[1:2, :]) + mod_ref[0:1, :]).astype(bf16)
        h_ref[...] = hb
        lr_ref[...] = jnp.dot(hb, wlr_ref[...], preferred_element_type=f32)

    acc = jnp.dot(h_ref[...], w_ref[...], preferred_element_type=f32)
    is_q = j < (COL_GV // PROJ_TN)
    is_k = j == jk

    @pl.when(is_q | is_k)
    def _():
        gain = jnp.where(is_q, qg_ref[...], kg_ref[...])
        post = jnp.where(is_q, HEAD_DIM ** -0.5, 1.0)
        cos = cos_ref[...]
        sin = sin_ref[...]
        for hh in range(PROJ_TN // HEAD_DIM):
            sl = slice(hh * HEAD_DIM, (hh + 1) * HEAD_DIM)
            n = _rms(acc[:, sl], gain)

            @pl.when(is_k)
            def _():
                kv_ref[:, sl] = n

            r = n * cos + pltpu.roll(n, HEAD_DIM // 2, 1) * sin
            main_ref[:, sl] = (r * post).astype(bf16)

    @pl.when(j == jv)
    def _():
        kv_ref[...] = acc
        main_ref[...] = acc.astype(bf16)

    @pl.when(jnp.logical_not(is_q | is_k | (j == jv)))
    def _():
        main_ref[...] = acc.astype(bf16)


def _in_proj(x_all, mod, layer, norm1, w_main, w_lr, qg, kg, cos_t, sin_t, n_prompt, dec_seq):
    t_all = x_all.shape[0]
    tm = _tile(min(n_prompt, dec_seq), 1024)
    cidx = _cond_index(tm, n_prompt, dec_seq)
    npt = n_prompt // tm
    per = dec_seq // tm
    ridx = lambda i: jnp.where(i < npt, per, (i - npt) % per)
    jk = COL_K // PROJ_TN
    jv = COL_V // PROJ_TN
    return pl.pallas_call(
        _in_proj_kernel,
        grid=(t_all // tm, MAIN_COLS // PROJ_TN),
        in_specs=[
            pl.BlockSpec((tm, D_MODEL), lambda i, j: (i, 0)),
            pl.BlockSpec((None, None, 6, D_MODEL), lambda i, j: (layer, cidx(i), 0, 0)),
            pl.BlockSpec((1, D_MODEL), lambda i, j: (0, 0)),
            pl.BlockSpec((D_MODEL, PROJ_TN), lambda i, j: (0, j)),
            pl.BlockSpec((D_MODEL, LANES), lambda i, j: (0, 0)),
            pl.BlockSpec((1, HEAD_DIM), lambda i, j: (0, 0)),
            pl.BlockSpec((1, HEAD_DIM), lambda i, j: (0, 0)),
            pl.BlockSpec((tm, HEAD_DIM), lambda i, j: (ridx(i), 0)),
            pl.BlockSpec((tm, HEAD_DIM), lambda i, j: (ridx(i), 0)),
        ],
        out_specs=[
            pl.BlockSpec((tm, PROJ_TN), lambda i, j: (i, j)),
            pl.BlockSpec((tm, LANES), lambda i, j: (i, 0)),
            pl.BlockSpec((tm, PROJ_TN), lambda i, j: (i, jnp.where(j == jv, 1, 0))),
        ],
        out_shape=[
            jax.ShapeDtypeStruct((t_all, MAIN_COLS), bf16),
            jax.ShapeDtypeStruct((t_all, LANES), f32),
            jax.ShapeDtypeStruct((t_all, 2 * PROJ_TN), f32),
        ],
        scratch_shapes=[pltpu.VMEM((tm, D_MODEL), bf16)],
        compiler_params=_params(("arbitrary", "arbitrary")),
        name="in_proj",
    )(x_all, mod, norm1, w_main, w_lr, qg, kg, cos_t, sin_t)


def _attn_kernel(*refs, has_ctx):
    if has_ctx:
        q_ref, k_ref, v_ref, ck_ref, cv_ref, _, o_ref = refs
        ck = ck_ref[...].astype(bf16)
        cv = cv_ref[...].astype(bf16)
    else:
        q_ref, k_ref, v_ref, _, o_ref = refs
    k = k_ref[...]
    v = v_ref[...]
    for g in range(Q_GROUPS):
        sl = slice(g * HEAD_DIM, (g + 1) * HEAD_DIM)
        q = q_ref[:, sl]
        s = lax.dot_general(q, k, NT_DIMS, preferred_element_type=f32)
        m = jnp.max(s, axis=-1, keepdims=True)
        if has_ctx:
            sc = lax.dot_general(q, ck, NT_DIMS, preferred_element_type=f32)
            m = jnp.maximum(m, jnp.max(sc, axis=-1, keepdims=True))
        p = jnp.exp(s - m)
        l = jnp.sum(p, axis=-1, keepdims=True)
        o = jnp.dot(p.astype(bf16), v, preferred_element_type=f32)
        if has_ctx:
            pc = jnp.exp(sc - m)
            l = l + jnp.sum(pc, axis=-1, keepdims=True)
            o = o + jnp.dot(pc.astype(bf16), cv, preferred_element_type=f32)
        o_ref[:, sl] = (o / l).astype(bf16)


def _attention(main, o_prev, row0, n_batch, seq, ctx=None):
    tq = _tile(seq, 256)
    nq = seq // tq
    assert row0 % seq == 0
    qb0 = row0 // tq
    kb0 = row0 // seq
    in_specs = [
        pl.BlockSpec((tq, PROJ_TN), lambda b, g, i: (qb0 + b * nq + i, g)),
        pl.BlockSpec((seq, HEAD_DIM), lambda b, g, i: (kb0 + b, COL_K // HEAD_DIM + g)),
        pl.BlockSpec((seq, HEAD_DIM), lambda b, g, i: (kb0 + b, COL_V // HEAD_DIM + g)),
    ]
    args = [main, main, main]
    if ctx is not None:
        ck, cv, layer = ctx
        past = ck.shape[2]
        spec = pl.BlockSpec((None, None, past, HEAD_DIM), lambda b, g, i: (b, layer, 0, g))
        in_specs += [spec, spec]
        args += [ck, cv]
    in_specs.append(pl.BlockSpec(memory_space=pl.ANY))
    args.append(o_prev)
    return pl.pallas_call(
        functools.partial(_attn_kernel, has_ctx=ctx is not None),
        grid=(n_batch, KV_HEADS, nq),
        in_specs=in_specs,
        out_specs=pl.BlockSpec((tq, PROJ_TN), lambda b, g, i: (qb0 + b * nq + i, g)),
        out_shape=jax.ShapeDtypeStruct(o_prev.shape, o_prev.dtype),
        input_output_aliases={len(args) - 1: 0},
        compiler_params=_params(("arbitrary", "arbitrary", "arbitrary")),
        name="attention_ctx" if ctx is not None else "attention",
    )(*args)


def _log_sigmoid(z):
    return jnp.minimum(z, 0.0) - jnp.log1p(jnp.exp(-jnp.abs(z)))


def _gla_chunk(q, k, v, g, st_ref, tri, mask, ref_row, last_row):
    g1 = g.astype(bf16)
    r1 = g - g1.astype(f32)
    g2 = r1.astype(bf16)
    g3 = (r1 - g2.astype(f32)).astype(bf16)
    b = (jnp.dot(tri, g1, preferred_element_type=f32) + jnp.dot(tri, g2, preferred_element_type=f32)
         + jnp.dot(tri, g3, preferred_element_type=f32))
    b_ref = b[ref_row:ref_row + 1, :]
    b_last = b[last_row:last_row + 1, :]
    st = st_ref[...]
    o = lax.dot_general((q * jnp.exp(b)).astype(bf16), st.astype(bf16), NT_DIMS, preferred_element_type=f32)
    qa = (q * jnp.exp(b - b_ref)).astype(bf16)
    ka = (k * jnp.exp(b_ref - b)).astype(bf16)
    a = lax.dot_general(qa, ka, NT_DIMS, preferred_element_type=f32)
    a = jnp.where(mask, a, 0.0).astype(bf16)
    o = o + jnp.dot(a, v, preferred_element_type=f32)
    kd = (k * jnp.exp(b_last - b)).astype(bf16)
    st_ref[...] = st * jnp.exp(b_last) + lax.dot_general(v, kd, TN_DIMS, preferred_element_type=f32)
    return o


def _gla_kernel(*refs, has_state, emit_state):
    (qf_ref, kf_ref, vf_ref, lf_ref, qb_ref, kb_ref, vb_ref, lb_ref,
     wf_ref, wb_ref, bf_ref, bb_ref) = refs[:12]
    refs = refs[12:]
    if has_state:
        s0f_ref, s0b_ref = refs[:2]
        refs = refs[2:]
    refs = refs[2:]
    of_ref, ob_ref = refs[:2]
    refs = refs[2:]
    if emit_state:
        sf_ref, sb_ref = refs[:2]
        refs = refs[2:]
    stf_ref, stb_ref = refs
    c = pl.program_id(2)

    @pl.when(c == 0)
    def _():
        if has_state:
            stf_ref[...] = s0f_ref[...].T
            stb_ref[...] = s0b_ref[...].T
        else:
            stf_ref[...] = jnp.zeros_like(stf_ref)
            stb_ref[...] = jnp.zeros_like(stb_ref)

    row = lax.broadcasted_iota(jnp.int32, (GLA_CHUNK, GLA_CHUNK), 0)
    col = lax.broadcasted_iota(jnp.int32, (GLA_CHUNK, GLA_CHUNK), 1)
    lower = col <= row
    upper = col >= row
    tri_f = jnp.where(lower, 1.0, 0.0).astype(bf16)
    tri_b = jnp.where(upper, 1.0, 0.0).astype(bf16)
    mid = GLA_CHUNK // 2
    n_chunks = GLA_BLOCK // GLA_CHUNK

    def decay(l_ref, w_ref, b_ref):
        z = jnp.dot(l_ref[...].astype(bf16), w_ref[...], preferred_element_type=f32) + b_ref[...]
        return _log_sigmoid(z) * (1.0 / GK_NORMALIZER)

    g_f = decay(lf_ref, wf_ref, bf_ref)
    g_b = decay(lb_ref, wb_ref, bb_ref)
    q_scale = GLA_DK ** -0.5
    for ci in range(n_chunks):
        sf = slice(ci * GLA_CHUNK, (ci + 1) * GLA_CHUNK)
        o = _gla_chunk(qf_ref[sf, :].astype(f32) * q_scale, kf_ref[sf, :].astype(f32), vf_ref[sf, :],
                       g_f[sf, :], stf_ref, tri_f, lower, mid, GLA_CHUNK - 1)
        of_ref[sf, :] = o.astype(of_ref.dtype)
        cb = n_chunks - 1 - ci
        sb = slice(cb * GLA_CHUNK, (cb + 1) * GLA_CHUNK)
        o = _gla_chunk(qb_ref[sb, :].astype(f32) * q_scale, kb_ref[sb, :].astype(f32), vb_ref[sb, :],
                       g_b[sb, :], stb_ref, tri_b, upper, GLA_CHUNK - 1 - mid, 0)
        ob_ref[sb, :] = o.astype(ob_ref.dtype)

    if emit_state:
        @pl.when(c == pl.num_programs(2) - 1)
        def _():
            sf_ref[...] = stf_ref[...].T
            sb_ref[...] = stb_ref[...].T


def _gla(main, lr, of_prev, ob_prev, wgk_f, wgk_b, bgk_f, bgk_b, row0, n_batch, seq,
         states=None, emit_state=False):
    nb = seq // GLA_BLOCK
    assert row0 % GLA_BLOCK == 0
    rb0 = row0 // GLA_BLOCK
    fwd = lambda b, c: rb0 + b * nb + c
    bwd = lambda b, c: rb0 + b * nb + (nb - 1 - c)

    def token_specs(pos):
        return [
            pl.BlockSpec((GLA_BLOCK, GLA_DK), lambda b, h, c: (pos(b, c), COL_GQ // GLA_DK + h)),
            pl.BlockSpec((GLA_BLOCK, GLA_DK), lambda b, h, c: (pos(b, c), COL_GK // GLA_DK + h)),
            pl.BlockSpec((GLA_BLOCK, GLA_DV), lambda b, h, c: (pos(b, c), COL_GV // GLA_DV + h)),
            pl.BlockSpec((GLA_BLOCK, LANES), lambda b, h, c: (pos(b, c), 0)),
        ]

    in_specs = token_specs(fwd) + token_specs(bwd) + [
        pl.BlockSpec((LANES, GLA_DK), lambda b, h, c: (0, h)),
        pl.BlockSpec((LANES, GLA_DK), lambda b, h, c: (0, h)),
        pl.BlockSpec((1, GLA_DK), lambda b, h, c: (0, h)),
        pl.BlockSpec((1, GLA_DK), lambda b, h, c: (0, h)),
    ]
    args = [main, main, main, lr, main, main, main, lr, wgk_f, wgk_b, bgk_f, bgk_b]
    if states is not None:
        s_f, s_b, layer = states
        spec = pl.BlockSpec((None, None, None, GLA_DK, GLA_DV), lambda b, h, c: (b, layer, h, 0, 0))
        in_specs += [spec, spec]
        args += [s_f, s_b]
    in_specs += [pl.BlockSpec(memory_space=pl.ANY)] * 2
    alias0 = len(args)
    args += [of_prev, ob_prev]
    out_specs = [
        pl.BlockSpec((GLA_BLOCK, GLA_DV), lambda b, h, c: (fwd(b, c), h)),
        pl.BlockSpec((GLA_BLOCK, GLA_DV), lambda b, h, c: (bwd(b, c), h)),
    ]
    out_shape = [jax.ShapeDtypeStruct(of_prev.shape, of_prev.dtype),
                 jax.ShapeDtypeStruct(ob_prev.shape, ob_prev.dtype)]
    if emit_state:
        spec = pl.BlockSpec((None, None, GLA_DK, GLA_DV), lambda b, h, c: (b, h, 0, 0))
        out_specs += [spec, spec]
        out_shape += [jax.ShapeDtypeStruct((n_batch, GLA_HEADS, GLA_DK, GLA_DV), f32)] * 2
    return pl.pallas_call(
        functools.partial(_gla_kernel, has_state=states is not None, emit_state=emit_state),
        grid=(n_batch, GLA_HEADS, nb),
        in_specs=in_specs,
        out_specs=out_specs,
        out_shape=out_shape,
        scratch_shapes=[pltpu.VMEM((GLA_DV, GLA_DK), f32), pltpu.VMEM((GLA_DV, GLA_DK), f32)],
        input_output_aliases={alias0: 0, alias0 + 1: 1},
        compiler_params=_params(("arbitrary", "arbitrary", "arbitrary")),
        name="gla_state" if states is not None else "gla",
    )(*args)


def _merge_kernel(x_ref, mod_ref, oa_ref, of_ref, ob_ref, gg_ref, ga_ref, gb_ref, gn_ref, w_ref,
                  o_ref, m_ref):
    for hh in range(GLA_HEADS):
        sl = slice(hh * GLA_DV, (hh + 1) * GLA_DV)
        og = of_ref[:, sl].astype(f32) + ob_ref[:, sl].astype(f32)
        gla = _rms(og, gn_ref[...]) * _silu(gg_ref[:, sl].astype(f32))
        merged = (jax.nn.sigmoid(ga_ref[:, sl].astype(f32)) * oa_ref[:, sl].astype(f32)
                  + jax.nn.sigmoid(gb_ref[:, sl].astype(f32)) * gla)
        m_ref[:, sl] = merged.astype(bf16)
    mix = jnp.dot(m_ref[...], w_ref[...], preferred_element_type=f32)
    o_ref[...] = x_ref[...] + mod_ref[2:3, :] * mix


def _merge_out(x_all, mod, layer, o_att, o_f, o_b, main, gla_norm, w_out, n_prompt, dec_seq):
    t_all = x_all.shape[0]
    tm = _tile(min(n_prompt, dec_seq), 256)
    cidx = _cond_index(tm, n_prompt, dec_seq)
    row = lambda i: (i, 0)
    return pl.pallas_call(
        _merge_kernel,
        grid=(t_all // tm,),
        in_specs=[
            pl.BlockSpec((tm, D_MODEL), row),
            pl.BlockSpec((None, None, 6, D_MODEL), lambda i: (layer, cidx(i), 0, 0)),
            pl.BlockSpec((tm, D_MODEL), row),
            pl.BlockSpec((tm, GLA_VAL), row),
            pl.BlockSpec((tm, GLA_VAL), row),
            pl.BlockSpec((tm, GLA_VAL), lambda i: (i, COL_GG // GLA_VAL)),
            pl.BlockSpec((tm, D_MODEL), lambda i: (i, COL_GA // D_MODEL)),
            pl.BlockSpec((tm, D_MODEL), lambda i: (i, COL_GB // D_MODEL)),
            pl.BlockSpec((1, GLA_DV), lambda i: (0, 0)),
            pl.BlockSpec((D_MODEL, D_MODEL), lambda i: (0, 0)),
        ],
        out_specs=pl.BlockSpec((tm, D_MODEL), row),
        out_shape=jax.ShapeDtypeStruct(x_all.shape, f32),
        scratch_shapes=[pltpu.VMEM((tm, D_MODEL), bf16)],
        compiler_params=_params(("arbitrary",)),
        name="merge_out",
    )(x_all, mod, o_att, o_f, o_b, main, main, main, gla_norm, w_out)


def _ffn_kernel(x_ref, mod_ref, n2_ref, wg_ref, wu_ref, wd_ref, o_ref, h_ref, acc_ref):
    j = pl.program_id(1)

    @pl.when(j == 0)
    def _():
        y = _rms(x_ref[...], n2_ref[...])
        h_ref[...] = (y * (1.0 + mod_ref[4:5, :]) + mod_ref[3:4, :]).astype(bf16)
        acc_ref[...] = jnp.zeros_like(acc_ref)

    h = h_ref[...]
    gate = jnp.dot(h, wg_ref[...], preferred_element_type=f32)
    up = jnp.dot(h, wu_ref[...], preferred_element_type=f32)
    acc_ref[...] += jnp.dot((_silu(gate) * up).astype(bf16), wd_ref[...], preferred_element_type=f32)

    @pl.when(j == pl.num_programs(1) - 1)
    def _():
        o_ref[...] = x_ref[...] + mod_ref[5:6, :] * acc_ref[...]


def _dense_ffn(x_all, mod, layer, norm2, wg, wu, wd, n_prompt, dec_seq):
    t_all = x_all.shape[0]
    tm = _tile(min(n_prompt, dec_seq), 512)
    tf = 512
    cidx = _cond_index(tm, n_prompt, dec_seq)
    return pl.pallas_call(
        _ffn_kernel,
        grid=(t_all // tm, FF_DENSE // tf),
        in_specs=[
            pl.BlockSpec((tm, D_MODEL), lambda i, j: (i, 0)),
            pl.BlockSpec((None, None, 6, D_MODEL), lambda i, j: (layer, cidx(i), 0, 0)),
            pl.BlockSpec((1, D_MODEL), lambda i, j: (0, 0)),
            pl.BlockSpec((D_MODEL, tf), lambda i, j: (0, j)),
            pl.BlockSpec((D_MODEL, tf), lambda i, j: (0, j)),
            pl.BlockSpec((tf, D_MODEL), lambda i, j: (j, 0)),
        ],
        out_specs=pl.BlockSpec((tm, D_MODEL), lambda i, j: (i, 0)),
        out_shape=jax.ShapeDtypeStruct(x_all.shape, f32),
        scratch_shapes=[pltpu.VMEM((tm, D_MODEL), bf16), pltpu.VMEM((tm, D_MODEL), f32)],
        compiler_params=_params(("arbitrary", "arbitrary")),
        name="dense_ffn",
    )(x_all, mod, norm2, wg, wu, wd)


INFO_E1, INFO_E2, INFO_W1, INFO_W2, INFO_R1, INFO_R2 = range(6)


def _route_kernel(x_ref, mod_ref, n2_ref, wr_ref, h_ref, info_ref, cnt_ref, run_ref):
    i = pl.program_id(0)
    tm = x_ref.shape[0]

    @pl.when(i == 0)
    def _():
        run_ref[...] = jnp.zeros_like(run_ref)

    y = _rms(x_ref[...], n2_ref[...])
    h = y * (1.0 + mod_ref[4:5, :]) + mod_ref[3:4, :]
    h_ref[...] = h
    logits = jnp.dot(h, wr_ref[...], preferred_element_type=f32, precision=lax.Precision.HIGHEST)
    lane = lax.broadcasted_iota(jnp.int32, (tm, LANES), 1)
    lane_f = lane.astype(f32)
    neg = -jnp.inf
    lg = jnp.where(lane < N_EXPERTS, logits, neg)
    m1 = jnp.max(lg, axis=-1, keepdims=True)
    i1 = jnp.min(jnp.where(lg == m1, lane_f, float(LANES)), axis=-1, keepdims=True)
    e1 = lane_f == i1
    lg2 = jnp.where(e1, neg, lg)
    m2 = jnp.max(lg2, axis=-1, keepdims=True)
    i2 = jnp.min(jnp.where(lg2 == m2, lane_f, float(LANES)), axis=-1, keepdims=True)
    e2 = lane_f == i2
    ex = jnp.exp(m2 - m1)
    w1 = 1.0 / (1.0 + ex)
    w2 = ex / (1.0 + ex)
    e1f = jnp.where(e1, 1.0, 0.0)
    e2f = jnp.where(e2, 1.0, 0.0)
    both = e1f + e2f
    row = lax.broadcasted_iota(jnp.int32, (tm, tm), 0)
    col = lax.broadcasted_iota(jnp.int32, (tm, tm), 1)
    strict = jnp.where(col < row, 1.0, 0.0).astype(bf16)
    before = jnp.dot(strict, both.astype(bf16), preferred_element_type=f32) + run_ref[...]
    r1 = jnp.sum(before * e1f, axis=-1, keepdims=True)
    r2 = jnp.sum(before * e2f, axis=-1, keepdims=True)
    run = run_ref[...] + jnp.sum(both, axis=0, keepdims=True)
    run_ref[...] = run
    cnt_ref[...] = run
    info = jnp.zeros((tm, LANES), f32)
    for idx, val in ((INFO_E1, i1), (INFO_E2, i2), (INFO_W1, w1), (INFO_W2, w2),
                     (INFO_R1, r1), (INFO_R2, r2)):
        info = jnp.where(lane == idx, val, info)
    info_ref[...] = info


def _route(x_all, mod, layer, norm2, w_router, n_prompt, dec_seq):
    t_all = x_all.shape[0]
    tm = _tile(min(n_prompt, dec_seq), 512)
    cidx = _cond_index(tm, n_prompt, dec_seq)
    return pl.pallas_call(
        _route_kernel,
        grid=(t_all // tm,),
        in_specs=[
            pl.BlockSpec((tm, D_MODEL), lambda i: (i, 0)),
            pl.BlockSpec((None, None, 6, D_MODEL), lambda i: (layer, cidx(i), 0, 0)),
            pl.BlockSpec((1, D_MODEL), lambda i: (0, 0)),
            pl.BlockSpec((D_MODEL, LANES), lambda i: (0, 0)),
        ],
        out_specs=[
            pl.BlockSpec((tm, D_MODEL), lambda i: (i, 0)),
            pl.BlockSpec((tm, LANES), lambda i: (i, 0)),
            pl.BlockSpec((1, LANES), lambda i: (0, 0)),
        ],
        out_shape=[
            jax.ShapeDtypeStruct((t_all, D_MODEL), f32),
            jax.ShapeDtypeStruct((t_all, LANES), f32),
            jax.ShapeDtypeStruct((1, LANES), f32),
        ],
        scratch_shapes=[pltpu.VMEM((1, LANES), f32)],
        compiler_params=_params(("arbitrary",)),
        name="route",
    )(x_all, mod, norm2, w_router)


def _row_copy(src, src_row, dst, dst_row, sem):
    return pltpu.make_async_copy(src.at[pl.ds(src_row, 1)], dst.at[pl.ds(dst_row, 1)], sem)


def _dispatch_kernel(dest_ref, h_hbm, _, out_hbm, sem, *, rows):
    base = pl.program_id(0) * rows

    def issue(r, carry):
        for kk in range(2):
            _row_copy(h_hbm, base + r, out_hbm, dest_ref[2 * (base + r) + kk], sem).start()
        return carry

    lax.fori_loop(0, rows, issue, 0)

    def drain(r, carry):
        for kk in range(2):
            _row_copy(h_hbm, 0, out_hbm, 0, sem).wait()
        return carry

    lax.fori_loop(0, rows, drain, 0)


def _dispatch(dest, h_all, n_rows):
    t_all = h_all.shape[0]
    rows = _tile(t_all, 512)
    zeros = jnp.zeros((n_rows, D_MODEL), f32)
    return pl.pallas_call(
        functools.partial(_dispatch_kernel, rows=rows),
        grid_spec=pltpu.PrefetchScalarGridSpec(
            num_scalar_prefetch=1,
            grid=(t_all // rows,),
            in_specs=[pl.BlockSpec(memory_space=pl.ANY), pl.BlockSpec(memory_space=pl.ANY)],
            out_specs=pl.BlockSpec(memory_space=pl.ANY),
            scratch_shapes=[pltpu.SemaphoreType.DMA(())],
        ),
        out_shape=jax.ShapeDtypeStruct((n_rows, D_MODEL), f32),
        input_output_aliases={2: 0},
        compiler_params=pltpu.CompilerParams(dimension_semantics=("arbitrary",), has_side_effects=True),
        name="moe_dispatch",
    )(dest, h_all, zeros)


def _expert_kernel(be_ref, na_ref, x_ref, wg_ref, wu_ref, wd_ref, o_ref, h_ref, acc_ref):
    b = pl.program_id(0)
    j = pl.program_id(1)
    active = b < na_ref[0]

    @pl.when(active & (j == 0))
    def _():
        h_ref[...] = x_ref[...].astype(bf16)
        acc_ref[...] = jnp.zeros_like(acc_ref)

    @pl.when(active)
    def _():
        h = h_ref[...]
        gate = jnp.dot(h, wg_ref[...], preferred_element_type=f32)
        up = jnp.dot(h, wu_ref[...], preferred_element_type=f32)
        acc_ref[...] += jnp.dot((_silu(gate) * up).astype(bf16), wd_ref[...], preferred_element_type=f32)

    @pl.when(j == pl.num_programs(1) - 1)
    def _():
        o_ref[...] = jnp.where(active, acc_ref[...], 0.0)


def _expert_ffn(block_expert, n_active, sorted_h, wg, wu, wd, tmb):
    n_rows = sorted_h.shape[0]
    tf = 512
    nf = FF_EXPERT // tf

    def ff(b, j, na):
        return jnp.where(b < na[0], j, nf - 1)

    return pl.pallas_call(
        _expert_kernel,
        grid_spec=pltpu.PrefetchScalarGridSpec(
            num_scalar_prefetch=2,
            grid=(n_rows // tmb, nf),
            in_specs=[
                pl.BlockSpec((tmb, D_MODEL), lambda b, j, be, na: (b, 0)),
                pl.BlockSpec((None, D_MODEL, tf), lambda b, j, be, na: (be[b], 0, ff(b, j, na))),
                pl.BlockSpec((None, D_MODEL, tf), lambda b, j, be, na: (be[b], 0, ff(b, j, na))),
                pl.BlockSpec((None, tf, D_MODEL), lambda b, j, be, na: (be[b], ff(b, j, na), 0)),
            ],
            out_specs=pl.BlockSpec((tmb, D_MODEL), lambda b, j, be, na: (b, 0)),
            scratch_shapes=[pltpu.VMEM((tmb, D_MODEL), bf16), pltpu.VMEM((tmb, D_MODEL), f32)],
        ),
        out_shape=jax.ShapeDtypeStruct((n_rows, D_MODEL), f32),
        compiler_params=_params(("arbitrary", "arbitrary")),
        name="expert_ffn",
    )(block_expert, n_active, sorted_h, wg, wu, wd)


def _combine_kernel(dest_ref, x_ref, mod_ref, info_ref, y_hbm, o_ref, buf_ref, sem, *, rows):
    base = pl.program_id(0) * rows

    def issue(r, carry):
        for kk in range(2):
            _row_copy(y_hbm, dest_ref[2 * (base + r) + kk], buf_ref.at[kk], r, sem).start()
        return carry

    lax.fori_loop(0, rows, issue, 0)

    def drain(r, carry):
        for kk in range(2):
            _row_copy(y_hbm, 0, buf_ref.at[kk], 0, sem).wait()
        return carry

    lax.fori_loop(0, rows, drain, 0)
    info = info_ref[...]
    w1 = info[:, INFO_W1:INFO_W1 + 1]
    w2 = info[:, INFO_W2:INFO_W2 + 1]
    o_ref[...] = x_ref[...] + mod_ref[5:6, :] * (w1 * buf_ref[0] + w2 * buf_ref[1])


def _combine(dest, x_all, mod, layer, info, y_sorted, n_prompt, dec_seq):
    t_all = x_all.shape[0]
    rows = _tile(min(n_prompt, dec_seq), 256)
    cidx = _cond_index(rows, n_prompt, dec_seq)
    return pl.pallas_call(
        functools.partial(_combine_kernel, rows=rows),
        grid_spec=pltpu.PrefetchScalarGridSpec(
            num_scalar_prefetch=1,
            grid=(t_all // rows,),
            in_specs=[
                pl.BlockSpec((rows, D_MODEL), lambda i, d: (i, 0)),
                pl.BlockSpec((None, None, 6, D_MODEL), lambda i, d: (layer, cidx(i), 0, 0)),
                pl.BlockSpec((rows, LANES), lambda i, d: (i, 0)),
                pl.BlockSpec(memory_space=pl.ANY),
            ],
            out_specs=pl.BlockSpec((rows, D_MODEL), lambda i, d: (i, 0)),
            scratch_shapes=[pltpu.VMEM((2, rows, D_MODEL), f32), pltpu.SemaphoreType.DMA(())],
        ),
        out_shape=jax.ShapeDtypeStruct(x_all.shape, f32),
        compiler_params=_params(("arbitrary",)),
        name="moe_combine",
    )(dest, x_all, mod, info, y_sorted)


def _moe_ffn(x_all, mod, layer, norm2, w_router, wg, wu, wd, n_prompt, dec_seq):
    t_all = x_all.shape[0]
    tmb = _tile(t_all, 512)
    n_blocks = (2 * t_all) // tmb + N_EXPERTS
    h_all, info, counts = _route(x_all, mod, layer, norm2, w_router, n_prompt, dec_seq)
    counts = counts[0, :N_EXPERTS].astype(jnp.int32)
    padded = ((counts + tmb - 1) // tmb) * tmb
    end = jnp.cumsum(padded)
    start = end - padded
    e12 = info[:, INFO_E1:INFO_E2 + 1].astype(jnp.int32)
    r12 = info[:, INFO_R1:INFO_R2 + 1].astype(jnp.int32)
    dest = (start[e12] + r12).reshape(2 * t_all)
    n_active = (end[-1] // tmb).astype(jnp.int32).reshape(1)
    blk = jnp.minimum(jnp.arange(n_blocks, dtype=jnp.int32), n_active[0] - 1) * tmb
    block_expert = jnp.minimum(jnp.searchsorted(end, blk, side="right"), N_EXPERTS - 1).astype(jnp.int32)
    sorted_h = _dispatch(dest, h_all, n_blocks * tmb)
    y_sorted = _expert_ffn(block_expert, n_active, sorted_h, wg, wu, wd, tmb)
    return _combine(dest, x_all, mod, layer, info, y_sorted, n_prompt, dec_seq)


def _final_kernel(x_ref, g_ref, o_ref):
    o_ref[...] = _rms(x_ref[...], g_ref[...])


def _final_norm(x_all, gain, row0, n_rows):
    tm = _tile(n_rows, 512)
    assert row0 % tm == 0
    return pl.pallas_call(
        _final_kernel,
        grid=(n_rows // tm,),
        in_specs=[pl.BlockSpec((tm, D_MODEL), lambda i: (row0 // tm + i, 0)),
                  pl.BlockSpec((1, D_MODEL), lambda i: (0, 0))],
        out_specs=pl.BlockSpec((tm, D_MODEL), lambda i: (i, 0)),
        out_shape=jax.ShapeDtypeStruct((n_rows, D_MODEL), f32),
        compiler_params=_params(("arbitrary",)),
        name="final_norm",
    )(x_all, gain)


def _rope_tables(dec_seq, tile_rows):
    rows = dec_seq // GRID_W
    row = jnp.repeat(jnp.arange(rows, dtype=f32), GRID_W)
    col = jnp.tile(jnp.arange(GRID_W, dtype=f32), rows)
    freqs = ROPE_THETA ** (-jnp.arange(AXIS_PAIRS, dtype=f32) / AXIS_PAIRS)
    ang = jnp.concatenate([row[:, None] * freqs, col[:, None] * freqs], axis=-1)
    cos, sin = jnp.cos(ang), jnp.sin(ang)
    cos_t = jnp.concatenate([cos, cos], axis=-1)
    sin_t = jnp.concatenate([-sin, sin], axis=-1)
    cos_t = jnp.concatenate([cos_t, jnp.ones((tile_rows, HEAD_DIM), f32)], axis=0)
    sin_t = jnp.concatenate([sin_t, jnp.zeros((tile_rows, HEAD_DIM), f32)], axis=0)
    return cos_t, sin_t


def _split_w_in(w):
    sizes = (ATT_HEADS * HEAD_DIM, KV_HEADS * HEAD_DIM, KV_HEADS * HEAD_DIM, GLA_KEY, GLA_KEY, GLA_VAL,
             GLA_VAL, GK_RANK, GK_RANK, D_MODEL, D_MODEL)
    offs = np.concatenate([[0], np.cumsum(sizes)])
    q, k, v, gq, gk, gv, gg, lrf, lrb, ga, gb = (w[:, offs[n]:offs[n + 1]] for n in range(len(sizes)))
    main = jnp.concatenate([q, gv, gg, ga, gb, gq, gk, k, v], axis=1).astype(bf16)
    lr = jnp.concatenate([lrf, lrb, jnp.zeros((D_MODEL, LANES - 2 * GK_RANK), w.dtype)], axis=1).astype(bf16)
    return main, lr


def kernel(x_prompt, x_sample, cache_k, cache_v, state_fwd, state_bwd, c, c_ctx, w_mod, b_mod, norm1, w_in,
           q_norm, k_norm, w_gk, b_gk, gla_norm, w_out, norm2, w_ff_gate, w_ff_up, w_ff_down, w_router,
           w_e_gate, w_e_up, w_e_down, final_norm):
    batch, seq, _ = x_prompt.shape
    dec_batch, dec_seq, _ = x_sample.shape
    depth = w_mod.shape[0]
    past = cache_k.shape[2]
    n_prompt = batch * seq
    n_sample = dec_batch * dec_seq
    t_all = n_prompt + n_sample
    assert dec_batch + 1 <= N_COND_PAD and n_prompt % dec_seq == 0 and seq % GLA_BLOCK == 0

    x_all = jnp.concatenate([x_prompt.reshape(n_prompt, D_MODEL), x_sample.reshape(n_sample, D_MODEL)], axis=0)
    cond = jnp.concatenate([c_ctx[None, :], c, jnp.zeros((N_COND_PAD - 1 - dec_batch, D_MODEL), f32)], axis=0)
    mod = _modulation(cond, w_mod, b_mod)

    proj_tm = _tile(min(n_prompt, dec_seq), 1024)
    cos_t, sin_t = _rope_tables(dec_seq, proj_tm)
    ck = cache_k.reshape(dec_batch, depth, past, KV_HEADS * HEAD_DIM)
    cv = cache_v.reshape(dec_batch, depth, past, KV_HEADS * HEAD_DIM)

    o_att = jnp.zeros((t_all, D_MODEL), bf16)
    o_f = jnp.zeros((t_all, GLA_VAL), bf16)
    o_b = jnp.zeros((t_all, GLA_VAL), bf16)
    new_k, new_v, new_sf, new_sb = [], [], [], []
    for l in range(depth):
        w_main, w_lr = _split_w_in(w_in[l])
        main, lr, kv32 = _in_proj(x_all, mod, l, norm1[l][None, :], w_main, w_lr, q_norm[l][None, :],
                                  k_norm[l][None, :], cos_t, sin_t, n_prompt, dec_seq)
        new_k.append(kv32[:n_prompt, :PROJ_TN].reshape(batch, seq, KV_HEADS, HEAD_DIM))
        new_v.append(kv32[:n_prompt, PROJ_TN:].reshape(batch, seq, KV_HEADS, HEAD_DIM))

        o_att = _attention(main, o_att, 0, batch, seq)
        o_att = _attention(main, o_att, n_prompt, dec_batch, dec_seq, ctx=(ck, cv, l))

        pad = jnp.zeros((LANES - GK_RANK, GLA_KEY), f32)
        wgk_f = jnp.concatenate([w_gk[l, 0], pad], axis=0).astype(bf16)
        wgk_b = jnp.concatenate([pad[:GK_RANK], w_gk[l, 1], pad[:LANES - 2 * GK_RANK]], axis=0).astype(bf16)
        bgk_f, bgk_b = b_gk[l, 0][None, :], b_gk[l, 1][None, :]
        o_f, o_b, s_f, s_b = _gla(main, lr, o_f, o_b, wgk_f, wgk_b, bgk_f, bgk_b, 0, batch, seq,
                                  emit_state=True)
        new_sf.append(s_f)
        new_sb.append(s_b)
        o_f, o_b = _gla(main, lr, o_f, o_b, wgk_f, wgk_b, bgk_f, bgk_b, n_prompt, dec_batch, dec_seq,
                        states=(state_fwd, state_bwd, l))

        x_all = _merge_out(x_all, mod, l, o_att, o_f, o_b, main, gla_norm[l][None, :], w_out[l].astype(bf16),
                           n_prompt, dec_seq)
        i = l // 2
        if l % 2 == 1:
            w_r = jnp.concatenate([w_router[i], jnp.zeros((D_MODEL, LANES - N_EXPERTS), f32)], axis=1)
            x_all = _moe_ffn(x_all, mod, l, norm2[l][None, :], w_r, w_e_gate[i].astype(bf16),
                             w_e_up[i].astype(bf16), w_e_down[i].astype(bf16), n_prompt, dec_seq)
        else:
            x_all = _dense_ffn(x_all, mod, l, norm2[l][None, :], w_ff_gate[i].astype(bf16),
                               w_ff_up[i].astype(bf16), w_ff_down[i].astype(bf16), n_prompt, dec_seq)

    y_prompt = _final_norm(x_all, final_norm[None, :], 0, n_prompt).reshape(batch, seq, D_MODEL)
    y_sample = _final_norm(x_all, final_norm[None, :], n_prompt, n_sample).reshape(dec_batch, dec_seq, D_MODEL)
    return (y_prompt, y_sample, jnp.stack(new_k, axis=1), jnp.stack(new_v, axis=1),
            jnp.stack(new_sf, axis=1), jnp.stack(new_sb, axis=1))
```

```python
import functools

import jax
import jax.numpy as jnp
import numpy as np
from jax import lax
from jax.experimental import pallas as pl
from jax.experimental.pallas import tpu as pltpu

D_MODEL = 2048
GRID_W = 64
ATT_HEADS = 16
KV_HEADS = 4
HEAD_DIM = 128
Q_GROUPS = ATT_HEADS // KV_HEADS
ROPE_THETA = 10000.0
AXIS_PAIRS = HEAD_DIM // 4
GLA_HEADS = 4
GLA_DK = D_MODEL // 8
GLA_DV = D_MODEL // 4
GLA_KEY = GLA_HEADS * GLA_DK
GLA_VAL = GLA_HEADS * GLA_DV
GK_RANK = 16
GK_NORMALIZER = 16.0
GLA_CHUNK = 64
FF_DENSE = ((8 * D_MODEL // 3 + 255) // 256) * 256
N_EXPERTS = 8
FF_EXPERT = 7 * D_MODEL // 2
EPS = 1e-6
LOG2_E = 1.4426950408889634

LANES = 128
N_COND_PAD = 16
VMEM_LIMIT = 56 * 1024 * 1024

COL_Q = 0
COL_GV = COL_Q + ATT_HEADS * HEAD_DIM
COL_GG = COL_GV + GLA_VAL
COL_GA = COL_GG + GLA_VAL
COL_GB = COL_GA + D_MODEL
COL_GQ = COL_GB + D_MODEL
COL_GK = COL_GQ + GLA_KEY
COL_K = COL_GK + GLA_KEY
COL_V = COL_K + KV_HEADS * HEAD_DIM
MAIN_COLS = COL_V + KV_HEADS * HEAD_DIM
PROJ_TN = Q_GROUPS * HEAD_DIM
GLA_BLOCK = 4 * GLA_CHUNK
GLA_HEADS_PER_STEP = 4

f32 = jnp.float32
bf16 = jnp.bfloat16
NT_DIMS = (((1,), (1,)), ((), ()))
TN_DIMS = (((0,), (0,)), ((), ()))


def _params(semantics):
    return pltpu.CompilerParams(dimension_semantics=semantics, vmem_limit_bytes=VMEM_LIMIT)


def _tile(n, pref):
    t = pref
    while n % t:
        t //= 2
    return t


def _cond_index(tile_rows, n_prompt, dec_seq):
    npt = n_prompt // tile_rows
    per = dec_seq // tile_rows
    return lambda i: jnp.where(i < npt, 0, 1 + (i - npt) // per)


def _silu(x):
    return x * jax.nn.sigmoid(x)


def _rms(x, gain):
    return x * lax.rsqrt(jnp.mean(x * x, axis=-1, keepdims=True) + EPS) * gain


def _mod_kernel(c_ref, w_ref, b_ref, o_ref):
    s = _silu(c_ref[...]).astype(bf16)
    o_ref[...] = jnp.dot(s, w_ref[...].astype(bf16), preferred_element_type=f32) + b_ref[...]


def _modulation(cond, w_mod, b_mod):
    depth = w_mod.shape[0]
    tn = 1024
    out = pl.pallas_call(
        _mod_kernel,
        grid=(depth, 6 * D_MODEL // tn),
        in_specs=[
            pl.BlockSpec((N_COND_PAD, D_MODEL), lambda l, n: (0, 0)),
            pl.BlockSpec((None, D_MODEL, tn), lambda l, n: (l, 0, n)),
            pl.BlockSpec((None, 1, tn), lambda l, n: (l, 0, n)),
        ],
        out_specs=pl.BlockSpec((None, N_COND_PAD, tn), lambda l, n: (l, 0, n)),
        out_shape=jax.ShapeDtypeStruct((depth, N_COND_PAD, 6 * D_MODEL), f32),
        compiler_params=_params(("arbitrary", "arbitrary")),
        name="modulation",
    )(cond, w_mod, b_mod.reshape(depth, 1, 6 * D_MODEL))
    return out.reshape(depth, N_COND_PAD, 6, D_MODEL)


def _in_proj_kernel(x_ref, mod_ref, n1_ref, w_ref, wlr_ref, qg_ref, kg_ref, cos_ref, sin_ref,
                    main_ref, lr_ref, kv_ref, h_ref):
    j = pl.program_id(1)
    jk = COL_K // PROJ_TN
    jv = COL_V // PROJ_TN

    @pl.when(j == 0)
    def _():
        y = _rms(x_ref[...], n1_ref[...])
        hb = (y * (1.0 + mod_ref[1:2, :]) + mod_ref[0:1, :]).astype(bf16)
        h_ref[...] = hb
        lr_ref[...] = jnp.dot(hb, wlr_ref[...], preferred_element_type=f32)

    acc = jnp.dot(h_ref[...], w_ref[...], preferred_element_type=f32)
    is_q = j < (COL_GV // PROJ_TN)
    is_k = j == jk

    @pl.when(is_q | is_k)
    def _():
        gain = jnp.where(is_q, qg_ref[...], kg_ref[...])
        post = jnp.where(is_q, HEAD_DIM ** -0.5 * LOG2_E, 1.0)
        cos = cos_ref[...]
        sin = sin_ref[...]
        for hh in range(PROJ_TN // HEAD_DIM):
            sl = slice(hh * HEAD_DIM, (hh + 1) * HEAD_DIM)
            n = _rms(acc[:, sl], gain)

            @pl.when(is_k)
            def _():
                kv_ref[:, sl] = n

            r = n * cos + pltpu.roll(n, HEAD_DIM // 2, 1) * sin
            main_ref[:, sl] = (r * post).astype(bf16)

    @pl.when(j == jv)
    def _():
        kv_ref[...] = acc
        main_ref[...] = acc.astype(bf16)

    @pl.when(jnp.logical_not(is_q | is_k | (j == jv)))
    def _():
        main_ref[...] = acc.astype(bf16)


def _in_proj(x_all, mod, layer, norm1, w_main, w_lr, qg, kg, cos_t, sin_t, n_prompt, dec_seq):
    t_all = x_all.shape[0]
    tm = _tile(min(n_prompt, dec_seq), 1024)
    cidx = _cond_index(tm, n_prompt, dec_seq)
    npt = n_prompt // tm
    per = dec_seq // tm
    ridx = lambda i: jnp.where(i < npt, per, (i - npt) % per)
    jk = COL_K // PROJ_TN
    jv = COL_V // PROJ_TN
    return pl.pallas_call(
        _in_proj_kernel,
        grid=(t_all // tm, MAIN_COLS // PROJ_TN),
        in_specs=[
            pl.BlockSpec((tm, D_MODEL), lambda i, j: (i, 0)),
            pl.BlockSpec((None, None, 6, D_MODEL), lambda i, j: (layer, cidx(i), 0, 0)),
            pl.BlockSpec((1, D_MODEL), lambda i, j: (0, 0)),
            pl.BlockSpec((D_MODEL, PROJ_TN), lambda i, j: (0, j)),
            pl.BlockSpec((D_MODEL, LANES), lambda i, j: (0, 0)),
            pl.BlockSpec((1, HEAD_DIM), lambda i, j: (0, 0)),
            pl.BlockSpec((1, HEAD_DIM), lambda i, j: (0, 0)),
            pl.BlockSpec((tm, HEAD_DIM), lambda i, j: (ridx(i), 0)),
            pl.BlockSpec((tm, HEAD_DIM), lambda i, j: (ridx(i), 0)),
        ],
        out_specs=[
            pl.BlockSpec((tm, PROJ_TN), lambda i, j: (i, j)),
            pl.BlockSpec((tm, LANES), lambda i, j: (i, 0)),
            pl.BlockSpec((tm, PROJ_TN), lambda i, j: (i, jnp.where(j == jv, 1, 0))),
        ],
        out_shape=[
            jax.ShapeDtypeStruct((t_all, MAIN_COLS), bf16),
            jax.ShapeDtypeStruct((t_all, LANES), f32),
            jax.ShapeDtypeStruct((t_all, 2 * PROJ_TN), f32),
        ],
        scratch_shapes=[pltpu.VMEM((tm, D_MODEL), bf16)],
        compiler_params=_params(("arbitrary", "arbitrary")),
        name="in_proj",
    )(x_all, mod, norm1, w_main, w_lr, qg, kg, cos_t, sin_t)


def _attn_kernel(*refs, has_ctx):
    if has_ctx:
        q_ref, k_ref, v_ref, ck_ref, cv_ref, _, o_ref, va_ref, ckb_ref, cva_ref = refs
    else:
        q_ref, k_ref, v_ref, _, o_ref, va_ref = refs

    @pl.when(pl.program_id(2) == 0)
    def _():
        va_ref[:, :HEAD_DIM] = v_ref[...]
        va_ref[:, HEAD_DIM:] = jnp.ones((va_ref.shape[0], HEAD_DIM), bf16)
        if has_ctx:
            ckb_ref[...] = ck_ref[...].astype(bf16)
            cva_ref[:, :HEAD_DIM] = cv_ref[...].astype(bf16)
            cva_ref[:, HEAD_DIM:] = jnp.ones((cva_ref.shape[0], HEAD_DIM), bf16)

    k = k_ref[...]

    def scores(g):
        q = q_ref[:, g * HEAD_DIM:(g + 1) * HEAD_DIM]
        s = lax.dot_general(q, k, NT_DIMS, preferred_element_type=f32)
        sc = lax.dot_general(q, ckb_ref[...], NT_DIMS, preferred_element_type=f32) if has_ctx else None
        return s, sc

    nxt = scores(0)
    for g in range(Q_GROUPS):
        s, sc = nxt
        if g + 1 < Q_GROUPS:
            nxt = scores(g + 1)
        m = jnp.max(s, axis=-1, keepdims=True)
        if has_ctx:
            m = jnp.maximum(m, jnp.max(sc, axis=-1, keepdims=True))
        oa = jnp.dot(jnp.exp2(s - m).astype(bf16), va_ref[...], preferred_element_type=f32)
        if has_ctx:
            oa = oa + jnp.dot(jnp.exp2(sc - m).astype(bf16), cva_ref[...], preferred_element_type=f32)
        o_ref[:, g * HEAD_DIM:(g + 1) * HEAD_DIM] = (oa[:, :HEAD_DIM] / oa[:, HEAD_DIM:]).astype(bf16)


def _attention(main, o_prev, row0, n_batch, seq, ctx=None):
    tq = _tile(seq, 256)
    nq = seq // tq
    assert row0 % seq == 0
    qb0 = row0 // tq
    kb0 = row0 // seq
    in_specs = [
        pl.BlockSpec((tq, PROJ_TN), lambda b, g, i: (qb0 + b * nq + i, g)),
        pl.BlockSpec((seq, HEAD_DIM), lambda b, g, i: (kb0 + b, COL_K // HEAD_DIM + g)),
        pl.BlockSpec((seq, HEAD_DIM), lambda b, g, i: (kb0 + b, COL_V // HEAD_DIM + g)),
    ]
    args = [main, main, main]
    if ctx is not None:
        ck, cv, layer = ctx
        past = ck.shape[2]
        spec = pl.BlockSpec((None, None, past, HEAD_DIM), lambda b, g, i: (b, layer, 0, g))
        in_specs += [spec, spec]
        args += [ck, cv]
    in_specs.append(pl.BlockSpec(memory_space=pl.ANY))
    args.append(o_prev)
    scratch = [pltpu.VMEM((seq, 2 * HEAD_DIM), bf16)]
    if ctx is not None:
        scratch += [pltpu.VMEM((past, HEAD_DIM), bf16), pltpu.VMEM((past, 2 * HEAD_DIM), bf16)]
    return pl.pallas_call(
        functools.partial(_attn_kernel, has_ctx=ctx is not None),
        grid=(n_batch, KV_HEADS, nq),
        in_specs=in_specs,
        out_specs=pl.BlockSpec((tq, PROJ_TN), lambda b, g, i: (qb0 + b * nq + i, g)),
        out_shape=jax.ShapeDtypeStruct(o_prev.shape, o_prev.dtype),
        scratch_shapes=scratch,
        input_output_aliases={len(args) - 1: 0},
        compiler_params=_params(("arbitrary", "arbitrary", "arbitrary")),
        name="attention_ctx" if ctx is not None else "attention",
    )(*args)


def _log_sigmoid(z):
    return jnp.minimum(z, 0.0) - jnp.log1p(jnp.exp(-jnp.abs(z)))


def _gla_chunk_prep(q, k, g, tri, mask, ref_row, last_row):
    g1 = g.astype(bf16)
    r1 = g - g1.astype(f32)
    g2 = r1.astype(bf16)
    g3 = (r1 - g2.astype(f32)).astype(bf16)
    b = (jnp.dot(tri, g1, preferred_element_type=f32) + jnp.dot(tri, g2, preferred_element_type=f32)
         + jnp.dot(tri, g3, preferred_element_type=f32))
    b_ref = b[ref_row:ref_row + 1, :]
    b_last = b[last_row:last_row + 1, :]
    qd = q * jnp.exp(b)
    kd = k * jnp.exp(b_last - b)
    qa = (q * jnp.exp(b - b_ref)).astype(bf16)
    ka = (k * jnp.exp(b_ref - b)).astype(bf16)
    a = lax.dot_general(qa, ka, NT_DIMS, preferred_element_type=f32)
    return qd, kd, jnp.where(mask, a, 0.0).astype(bf16), jnp.exp(b_last)


def _gla_block(q_ref, k_ref, v_ref, g, st_ref, o_ref, tri, mask, ref_row, last_row, order):
    n = len(order)
    q_scale = GLA_DK ** -0.5
    prep = {}
    for c in order:
        s = slice(c * GLA_CHUNK, (c + 1) * GLA_CHUNK)
        prep[c] = _gla_chunk_prep(q_ref[s, :].astype(f32) * q_scale, k_ref[s, :].astype(f32), g[s, :],
                                  tri, mask, ref_row, last_row)
    zero = jnp.zeros((GLA_CHUNK, GLA_CHUNK), bf16)
    blocks = [[zero] * n for _ in range(n)]
    q_rows = [None] * n
    k_rows = [None] * n
    into = None
    for pi, ci in enumerate(order):
        qd, _, a_diag, dec = prep[ci]
        blocks[ci][ci] = a_diag
        q_rows[ci] = (qd if into is None else qd * into).astype(bf16)
        between = None
        for pj in range(pi - 1, -1, -1):
            cj = order[pj]
            qs = (qd if between is None else qd * between).astype(bf16)
            blocks[ci][cj] = lax.dot_general(qs, prep[cj][1].astype(bf16), NT_DIMS,
                                             preferred_element_type=f32).astype(bf16)
            between = prep[cj][3] if between is None else between * prep[cj][3]
        into = dec if into is None else into * dec
    after = None
    for ci in reversed(order):
        kd = prep[ci][1]
        k_rows[ci] = (kd if after is None else kd * after).astype(bf16)
        after = prep[ci][3] if after is None else after * prep[ci][3]
    a_full = jnp.concatenate([jnp.concatenate(r, axis=1) for r in blocks], axis=0)
    q_blk = jnp.concatenate(q_rows, axis=0)
    k_blk = jnp.concatenate(k_rows, axis=0)
    v = v_ref[...]
    st = st_ref[...]
    o = (jnp.dot(a_full, v, preferred_element_type=f32)
         + lax.dot_general(q_blk, st.astype(bf16), NT_DIMS, preferred_element_type=f32))
    o_ref[...] = o.astype(o_ref.dtype)
    st_ref[...] = st * into + lax.dot_general(v, k_blk, TN_DIMS, preferred_element_type=f32)


def _gla_kernel(*refs, has_state, emit_state):
    (qf_ref, kf_ref, vf_ref, lf_ref, qb_ref, kb_ref, vb_ref, lb_ref,
     wf_ref, wb_ref, bf_ref, bb_ref) = refs[:12]
    refs = refs[12:]
    if has_state:
        s0f_ref, s0b_ref = refs[:2]
        refs = refs[2:]
    refs = refs[2:]
    of_ref, ob_ref = refs[:2]
    refs = refs[2:]
    if emit_state:
        sf_ref, sb_ref = refs[:2]
        refs = refs[2:]
    stf_ref, stb_ref = refs
    c = pl.program_id(2)
    heads = stf_ref.shape[0]

    @pl.when(c == 0)
    def _():
        for hh in range(heads):
            if has_state:
                stf_ref[hh] = s0f_ref[hh].T
                stb_ref[hh] = s0b_ref[hh].T
            else:
                stf_ref[hh] = jnp.zeros((GLA_DV, GLA_DK), f32)
                stb_ref[hh] = jnp.zeros((GLA_DV, GLA_DK), f32)

    row = lax.broadcasted_iota(jnp.int32, (GLA_CHUNK, GLA_CHUNK), 0)
    col = lax.broadcasted_iota(jnp.int32, (GLA_CHUNK, GLA_CHUNK), 1)
    lower = col <= row
    upper = col >= row
    tri_f = jnp.where(lower, 1.0, 0.0).astype(bf16)
    tri_b = jnp.where(upper, 1.0, 0.0).astype(bf16)
    mid = GLA_CHUNK // 2
    n_chunks = GLA_BLOCK // GLA_CHUNK

    def decay(l_ref, w_ref, b_ref):
        z = jnp.dot(l_ref[...].astype(bf16), w_ref[...], preferred_element_type=f32) + b_ref[...]
        return _log_sigmoid(z) * (1.0 / GK_NORMALIZER)

    g_f = decay(lf_ref, wf_ref, bf_ref)
    g_b = decay(lb_ref, wb_ref, bb_ref)
    order = list(range(n_chunks))
    for hh in range(heads):
        ks = slice(hh * GLA_DK, (hh + 1) * GLA_DK)
        vs = slice(hh * GLA_DV, (hh + 1) * GLA_DV)
        _gla_block(qf_ref.at[:, ks], kf_ref.at[:, ks], vf_ref.at[:, vs], g_f[:, ks], stf_ref.at[hh],
                   of_ref.at[:, vs], tri_f, lower, mid, GLA_CHUNK - 1, order)
        _gla_block(qb_ref.at[:, ks], kb_ref.at[:, ks], vb_ref.at[:, vs], g_b[:, ks], stb_ref.at[hh],
                   ob_ref.at[:, vs], tri_b, upper, GLA_CHUNK - 1 - mid, 0, order[::-1])

    if emit_state:
        @pl.when(c == pl.num_programs(2) - 1)
        def _():
            for hh in range(heads):
                sf_ref[hh] = stf_ref[hh].T
                sb_ref[hh] = stb_ref[hh].T


def _gla(main, lr, of_prev, ob_prev, wgk_f, wgk_b, bgk_f, bgk_b, row0, n_batch, seq,
         states=None, emit_state=False):
    nb = seq // GLA_BLOCK
    assert row0 % GLA_BLOCK == 0
    rb0 = row0 // GLA_BLOCK
    fwd = lambda b, c: rb0 + b * nb + c
    bwd = lambda b, c: rb0 + b * nb + (nb - 1 - c)

    hps = GLA_HEADS_PER_STEP
    dk, dv = hps * GLA_DK, hps * GLA_DV

    def token_specs(pos):
        return [
            pl.BlockSpec((GLA_BLOCK, dk), lambda b, h, c: (pos(b, c), COL_GQ // dk + h)),
            pl.BlockSpec((GLA_BLOCK, dk), lambda b, h, c: (pos(b, c), COL_GK // dk + h)),
            pl.BlockSpec((GLA_BLOCK, dv), lambda b, h, c: (pos(b, c), COL_GV // dv + h)),
            pl.BlockSpec((GLA_BLOCK, LANES), lambda b, h, c: (pos(b, c), 0)),
        ]

    in_specs = token_specs(fwd) + token_specs(bwd) + [
        pl.BlockSpec((LANES, dk), lambda b, h, c: (0, h)),
        pl.BlockSpec((LANES, dk), lambda b, h, c: (0, h)),
        pl.BlockSpec((1, dk), lambda b, h, c: (0, h)),
        pl.BlockSpec((1, dk), lambda b, h, c: (0, h)),
    ]
    args = [main, main, main, lr, main, main, main, lr, wgk_f, wgk_b, bgk_f, bgk_b]
    if states is not None:
        s_f, s_b, layer = states
        spec = pl.BlockSpec((None, None, hps, GLA_DK, GLA_DV), lambda b, h, c: (b, layer, h, 0, 0))
        in_specs += [spec, spec]
        args += [s_f, s_b]
    in_specs += [pl.BlockSpec(memory_space=pl.ANY)] * 2
    alias0 = len(args)
    args += [of_prev, ob_prev]
    out_specs = [
        pl.BlockSpec((GLA_BLOCK, dv), lambda b, h, c: (fwd(b, c), h)),
        pl.BlockSpec((GLA_BLOCK, dv), lambda b, h, c: (bwd(b, c), h)),
    ]
    out_shape = [jax.ShapeDtypeStruct(of_prev.shape, of_prev.dtype),
                 jax.ShapeDtypeStruct(ob_prev.shape, ob_prev.dtype)]
    if emit_state:
        spec = pl.BlockSpec((None, hps, GLA_DK, GLA_DV), lambda b, h, c: (b, h, 0, 0))
        out_specs += [spec, spec]
        out_shape += [jax.ShapeDtypeStruct((n_batch, GLA_HEADS, GLA_DK, GLA_DV), f32)] * 2
    return pl.pallas_call(
        functools.partial(_gla_kernel, has_state=states is not None, emit_state=emit_state),
        grid=(n_batch, GLA_HEADS // hps, nb),
        in_specs=in_specs,
        out_specs=out_specs,
        out_shape=out_shape,
        scratch_shapes=[pltpu.VMEM((hps, GLA_DV, GLA_DK), f32), pltpu.VMEM((hps, GLA_DV, GLA_DK), f32)],
        input_output_aliases={alias0: 0, alias0 + 1: 1},
        compiler_params=_params(("arbitrary", "arbitrary", "arbitrary")),
        name="gla_state" if states is not None else "gla",
    )(*args)


def _merge_kernel(x_ref, mod_ref, oa_ref, of_ref, ob_ref, gg_ref, ga_ref, gb_ref, gn_ref, w_ref,
                  o_ref, m_ref):
    for hh in range(GLA_HEADS):
        sl = slice(hh * GLA_DV, (hh + 1) * GLA_DV)
        og = of_ref[:, sl].astype(f32) + ob_ref[:, sl].astype(f32)
        gla = _rms(og, gn_ref[...]) * _silu(gg_ref[:, sl].astype(f32))
        merged = (jax.nn.sigmoid(ga_ref[:, sl].astype(f32)) * oa_ref[:, sl].astype(f32)
                  + jax.nn.sigmoid(gb_ref[:, sl].astype(f32)) * gla)
        m_ref[:, sl] = merged.astype(bf16)
    mix = jnp.dot(m_ref[...], w_ref[...], preferred_element_type=f32)
    o_ref[...] = x_ref[...] + mod_ref[2:3, :] * mix


def _merge_out(x_all, mod, layer, o_att, o_f, o_b, main, gla_norm, w_out, n_prompt, dec_seq):
    t_all = x_all.shape[0]
    tm = _tile(min(n_prompt, dec_seq), 256)
    cidx = _cond_index(tm, n_prompt, dec_seq)
    row = lambda i: (i, 0)
    return pl.pallas_call(
        _merge_kernel,
        grid=(t_all // tm,),
        in_specs=[
            pl.BlockSpec((tm, D_MODEL), row),
            pl.BlockSpec((None, None, 6, D_MODEL), lambda i: (layer, cidx(i), 0, 0)),
            pl.BlockSpec((tm, D_MODEL), row),
            pl.BlockSpec((tm, GLA_VAL), row),
            pl.BlockSpec((tm, GLA_VAL), row),
            pl.BlockSpec((tm, GLA_VAL), lambda i: (i, COL_GG // GLA_VAL)),
            pl.BlockSpec((tm, D_MODEL), lambda i: (i, COL_GA // D_MODEL)),
            pl.BlockSpec((tm, D_MODEL), lambda i: (i, COL_GB // D_MODEL)),
            pl.BlockSpec((1, GLA_DV), lambda i: (0, 0)),
            pl.BlockSpec((D_MODEL, D_MODEL), lambda i: (0, 0)),
        ],
        out_specs=pl.BlockSpec((tm, D_MODEL), row),
        out_shape=jax.ShapeDtypeStruct(x_all.shape, f32),
        scratch_shapes=[pltpu.VMEM((tm, D_MODEL), bf16)],
        compiler_params=_params(("arbitrary",)),
        name="merge_out",
    )(x_all, mod, o_att, o_f, o_b, main, main, main, gla_norm, w_out)


def _ffn_kernel(x_ref, mod_ref, n2_ref, wg_ref, wu_ref, wd_ref, o_ref, h_ref, acc_ref):
    j = pl.program_id(1)

    @pl.when(j == 0)
    def _():
        y = _rms(x_ref[...], n2_ref[...])
        h_ref[...] = (y * (1.0 + mod_ref[4:5, :]) + mod_ref[3:4, :]).astype(bf16)
        acc_ref[...] = jnp.zeros_like(acc_ref)

    h = h_ref[...]
    gate = jnp.dot(h, wg_ref[...], preferred_element_type=f32)
    up = jnp.dot(h, wu_ref[...], preferred_element_type=f32)
    acc_ref[...] += jnp.dot((_silu(gate) * up).astype(bf16), wd_ref[...], preferred_element_type=f32)

    @pl.when(j == pl.num_programs(1) - 1)
    def _():
        o_ref[...] = x_ref[...] + mod_ref[5:6, :] * acc_ref[...]


def _dense_ffn(x_all, mod, layer, norm2, wg, wu, wd, n_prompt, dec_seq):
    t_all = x_all.shape[0]
    tm = _tile(min(n_prompt, dec_seq), 512)
    tf = 512
    cidx = _cond_index(tm, n_prompt, dec_seq)
    return pl.pallas_call(
        _ffn_kernel,
        grid=(t_all // tm, FF_DENSE // tf),
        in_specs=[
            pl.BlockSpec((tm, D_MODEL), lambda i, j: (i, 0)),
            pl.BlockSpec((None, None, 6, D_MODEL), lambda i, j: (layer, cidx(i), 0, 0)),
            pl.BlockSpec((1, D_MODEL), lambda i, j: (0, 0)),
            pl.BlockSpec((D_MODEL, tf), lambda i, j: (0, j)),
            pl.BlockSpec((D_MODEL, tf), lambda i, j: (0, j)),
            pl.BlockSpec((tf, D_MODEL), lambda i, j: (j, 0)),
        ],
        out_specs=pl.BlockSpec((tm, D_MODEL), lambda i, j: (i, 0)),
        out_shape=jax.ShapeDtypeStruct(x_all.shape, f32),
        scratch_shapes=[pltpu.VMEM((tm, D_MODEL), bf16), pltpu.VMEM((tm, D_MODEL), f32)],
        compiler_params=_params(("arbitrary", "arbitrary")),
        name="dense_ffn",
    )(x_all, mod, norm2, wg, wu, wd)


INFO_E1, INFO_E2, INFO_W1, INFO_W2, INFO_R1, INFO_R2 = range(6)


def _route_kernel(x_ref, mod_ref, n2_ref, wr_ref, h_ref, info_ref, cnt_ref, run_ref):
    i = pl.program_id(0)
    tm = x_ref.shape[0]

    @pl.when(i == 0)
    def _():
        run_ref[...] = jnp.zeros_like(run_ref)

    y = _rms(x_ref[...], n2_ref[...])
    h = y * (1.0 + mod_ref[4:5, :]) + mod_ref[3:4, :]
    h_ref[...] = h
    logits = jnp.dot(h, wr_ref[...], preferred_element_type=f32, precision=lax.Precision.HIGHEST)
    lane = lax.broadcasted_iota(jnp.int32, (tm, LANES), 1)
    lane_f = lane.astype(f32)
    neg = -jnp.inf
    lg = jnp.where(lane < N_EXPERTS, logits, neg)
    m1 = jnp.max(lg, axis=-1, keepdims=True)
    i1 = jnp.min(jnp.where(lg == m1, lane_f, float(LANES)), axis=-1, keepdims=True)
    e1 = lane_f == i1
    lg2 = jnp.where(e1, neg, lg)
    m2 = jnp.max(lg2, axis=-1, keepdims=True)
    i2 = jnp.min(jnp.where(lg2 == m2, lane_f, float(LANES)), axis=-1, keepdims=True)
    e2 = lane_f == i2
    ex = jnp.exp(m2 - m1)
    w1 = 1.0 / (1.0 + ex)
    w2 = ex / (1.0 + ex)
    e1f = jnp.where(e1, 1.0, 0.0)
    e2f = jnp.where(e2, 1.0, 0.0)
    both = e1f + e2f
    row = lax.broadcasted_iota(jnp.int32, (tm, tm), 0)
    col = lax.broadcasted_iota(jnp.int32, (tm, tm), 1)
    strict = jnp.where(col < row, 1.0, 0.0).astype(bf16)
    before = jnp.dot(strict, both.astype(bf16), preferred_element_type=f32) + run_ref[...]
    r1 = jnp.sum(before * e1f, axis=-1, keepdims=True)
    r2 = jnp.sum(before * e2f, axis=-1, keepdims=True)
    run = run_ref[...] + jnp.sum(both, axis=0, keepdims=True)
    run_ref[...] = run
    cnt_ref[...] = run
    info = jnp.zeros((tm, LANES), f32)
    for idx, val in ((INFO_E1, i1), (INFO_E2, i2), (INFO_W1, w1), (INFO_W2, w2),
                     (INFO_R1, r1), (INFO_R2, r2)):
        info = jnp.where(lane == idx, val, info)
    info_ref[...] = info


def _route(x_all, mod, layer, norm2, w_router, n_prompt, dec_seq):
    t_all = x_all.shape[0]
    tm = _tile(min(n_prompt, dec_seq), 512)
    cidx = _cond_index(tm, n_prompt, dec_seq)
    return pl.pallas_call(
        _route_kernel,
        grid=(t_all // tm,),
        in_specs=[
            pl.BlockSpec((tm, D_MODEL), lambda i: (i, 0)),
            pl.BlockSpec((None, None, 6, D_MODEL), lambda i: (layer, cidx(i), 0, 0)),
            pl.BlockSpec((1, D_MODEL), lambda i: (0, 0)),
            pl.BlockSpec((D_MODEL, LANES), lambda i: (0, 0)),
        ],
        out_specs=[
            pl.BlockSpec((tm, D_MODEL), lambda i: (i, 0)),
            pl.BlockSpec((tm, LANES), lambda i: (i, 0)),
            pl.BlockSpec((1, LANES), lambda i: (0, 0)),
        ],
        out_shape=[
            jax.ShapeDtypeStruct((t_all, D_MODEL), f32),
            jax.ShapeDtypeStruct((t_all, LANES), f32),
            jax.ShapeDtypeStruct((1, LANES), f32),
        ],
        scratch_shapes=[pltpu.VMEM((1, LANES), f32)],
        compiler_params=_params(("arbitrary",)),
        name="route",
    )(x_all, mod, norm2, w_router)


def _row_copy(src, src_row, dst, dst_row, sem):
    return pltpu.make_async_copy(src.at[pl.ds(src_row, 1)], dst.at[pl.ds(dst_row, 1)], sem)


def _dispatch_kernel(tok_ref, h_hbm, o_ref, buf_ref, sem, *, rows):
    base = pl.program_id(0) * rows

    def issue(r, carry):
        _row_copy(h_hbm, tok_ref[base + r], buf_ref, r, sem).start()
        return carry

    lax.fori_loop(0, rows, issue, 0)

    def drain(r, carry):
        _row_copy(h_hbm, 0, buf_ref, 0, sem).wait()
        return carry

    lax.fori_loop(0, rows, drain, 0)
    o_ref[...] = buf_ref[...].astype(bf16)


def _dispatch(slot_token, h_all):
    n_rows = slot_token.shape[0]
    rows = _tile(n_rows, 256)
    return pl.pallas_call(
        functools.partial(_dispatch_kernel, rows=rows),
        grid_spec=pltpu.PrefetchScalarGridSpec(
            num_scalar_prefetch=1,
            grid=(n_rows // rows,),
            in_specs=[pl.BlockSpec(memory_space=pl.ANY)],
            out_specs=pl.BlockSpec((rows, D_MODEL), lambda i, t: (i, 0)),
            scratch_shapes=[pltpu.VMEM((rows, D_MODEL), f32), pltpu.SemaphoreType.DMA(())],
        ),
        out_shape=jax.ShapeDtypeStruct((n_rows, D_MODEL), bf16),
        compiler_params=_params(("arbitrary",)),
        name="moe_dispatch",
    )(slot_token, h_all)


def _expert_kernel(be_ref, na_ref, x_ref, wg_ref, wu_ref, wd_ref, o_ref, acc_ref):
    b = pl.program_id(0)
    j = pl.program_id(1)
    active = b < na_ref[0]

    @pl.when(active & (j == 0))
    def _():
        acc_ref[...] = jnp.zeros_like(acc_ref)

    @pl.when(active)
    def _():
        h = x_ref[...]
        gate = jnp.dot(h, wg_ref[...], preferred_element_type=f32)
        up = jnp.dot(h, wu_ref[...], preferred_element_type=f32)
        acc_ref[...] += jnp.dot((_silu(gate) * up).astype(bf16), wd_ref[...], preferred_element_type=f32)

    @pl.when(j == pl.num_programs(1) - 1)
    def _():
        o_ref[...] = jnp.where(active, acc_ref[...], 0.0)


def _expert_ffn(block_expert, n_active, sorted_h, wg, wu, wd, tmb):
    n_rows = sorted_h.shape[0]
    tf = 512
    nf = FF_EXPERT // tf

    def ff(b, j, na):
        return jnp.where(b < na[0], j, nf - 1)

    return pl.pallas_call(
        _expert_kernel,
        grid_spec=pltpu.PrefetchScalarGridSpec(
            num_scalar_prefetch=2,
            grid=(n_rows // tmb, nf),
            in_specs=[
                pl.BlockSpec((tmb, D_MODEL), lambda b, j, be, na: (b, 0)),
                pl.BlockSpec((None, D_MODEL, tf), lambda b, j, be, na: (be[b], 0, ff(b, j, na))),
                pl.BlockSpec((None, D_MODEL, tf), lambda b, j, be, na: (be[b], 0, ff(b, j, na))),
                pl.BlockSpec((None, tf, D_MODEL), lambda b, j, be, na: (be[b], ff(b, j, na), 0)),
            ],
            out_specs=pl.BlockSpec((tmb, D_MODEL), lambda b, j, be, na: (b, 0)),
            scratch_shapes=[pltpu.VMEM((tmb, D_MODEL), f32)],
        ),
        out_shape=jax.ShapeDtypeStruct((n_rows, D_MODEL), f32),
        compiler_params=_params(("arbitrary", "arbitrary")),
        name="expert_ffn",
    )(block_expert, n_active, sorted_h, wg, wu, wd)


def _combine_kernel(dest_ref, x_ref, mod_ref, info_ref, y_hbm, o_ref, buf_ref, sem, *, rows):
    base = pl.program_id(0) * rows

    def issue(r, carry):
        for kk in range(2):
            _row_copy(y_hbm, dest_ref[2 * (base + r) + kk], buf_ref.at[kk], r, sem).start()
        return carry

    lax.fori_loop(0, rows, issue, 0)

    def drain(r, carry):
        for kk in range(2):
            _row_copy(y_hbm, 0, buf_ref.at[kk], 0, sem).wait()
        return carry

    lax.fori_loop(0, rows, drain, 0)
    info = info_ref[...]
    w1 = info[:, INFO_W1:INFO_W1 + 1]
    w2 = info[:, INFO_W2:INFO_W2 + 1]
    o_ref[...] = x_ref[...] + mod_ref[5:6, :] * (w1 * buf_ref[0] + w2 * buf_ref[1])


def _combine(dest, x_all, mod, layer, info, y_sorted, n_prompt, dec_seq):
    t_all = x_all.shape[0]
    rows = _tile(min(n_prompt, dec_seq), 256)
    cidx = _cond_index(rows, n_prompt, dec_seq)
    return pl.pallas_call(
        functools.partial(_combine_kernel, rows=rows),
        grid_spec=pltpu.PrefetchScalarGridSpec(
            num_scalar_prefetch=1,
            grid=(t_all // rows,),
            in_specs=[
                pl.BlockSpec((rows, D_MODEL), lambda i, d: (i, 0)),
                pl.BlockSpec((None, None, 6, D_MODEL), lambda i, d: (layer, cidx(i), 0, 0)),
                pl.BlockSpec((rows, LANES), lambda i, d: (i, 0)),
                pl.BlockSpec(memory_space=pl.ANY),
            ],
            out_specs=pl.BlockSpec((rows, D_MODEL), lambda i, d: (i, 0)),
            scratch_shapes=[pltpu.VMEM((2, rows, D_MODEL), f32), pltpu.SemaphoreType.DMA(())],
        ),
        out_shape=jax.ShapeDtypeStruct(x_all.shape, f32),
        compiler_params=_params(("arbitrary",)),
        name="moe_combine",
    )(dest, x_all, mod, info, y_sorted)


def _moe_ffn(x_all, mod, layer, norm2, w_router, wg, wu, wd, n_prompt, dec_seq):
    t_all = x_all.shape[0]
    tmb = _tile(t_all, 512)
    n_blocks = (2 * t_all) // tmb + N_EXPERTS
    h_all, info, counts = _route(x_all, mod, layer, norm2, w_router, n_prompt, dec_seq)
    counts = counts[0, :N_EXPERTS].astype(jnp.int32)
    padded = ((counts + tmb - 1) // tmb) * tmb
    end = jnp.cumsum(padded)
    start = end - padded
    e12 = info[:, INFO_E1:INFO_E2 + 1].astype(jnp.int32)
    r12 = info[:, INFO_R1:INFO_R2 + 1].astype(jnp.int32)
    dest = (start[e12] + r12).reshape(2 * t_all)
    slot_token = jnp.zeros((n_blocks * tmb,), jnp.int32).at[dest].set(
        jnp.arange(2 * t_all, dtype=jnp.int32) // 2, unique_indices=True)
    n_active = (end[-1] // tmb).astype(jnp.int32).reshape(1)
    blk = jnp.minimum(jnp.arange(n_blocks, dtype=jnp.int32), n_active[0] - 1) * tmb
    block_expert = jnp.minimum(jnp.sum(blk[:, None] >= end[None, :], axis=1), N_EXPERTS - 1).astype(jnp.int32)
    sorted_h = _dispatch(slot_token, h_all)
    y_sorted = _expert_ffn(block_expert, n_active, sorted_h, wg, wu, wd, tmb)
    return _combine(dest, x_all, mod, layer, info, y_sorted, n_prompt, dec_seq)


def _final_kernel(x_ref, g_ref, o_ref):
    o_ref[...] = _rms(x_ref[...], g_ref[...])


def _final_norm(x_all, gain, row0, n_rows):
    tm = _tile(n_rows, 512)
    assert row0 % tm == 0
    return pl.pallas_call(
        _final_kernel,
        grid=(n_rows // tm,),
        in_specs=[pl.BlockSpec((tm, D_MODEL), lambda i: (row0 // tm + i, 0)),
                  pl.BlockSpec((1, D_MODEL), lambda i: (0, 0))],
        out_specs=pl.BlockSpec((tm, D_MODEL), lambda i: (i, 0)),
        out_shape=jax.ShapeDtypeStruct((n_rows, D_MODEL), f32),
        compiler_params=_params(("arbitrary",)),
        name="final_norm",
    )(x_all, gain)


def _rope_tables(dec_seq, tile_rows):
    rows = dec_seq // GRID_W
    row = jnp.repeat(jnp.arange(rows, dtype=f32), GRID_W)
    col = jnp.tile(jnp.arange(GRID_W, dtype=f32), rows)
    freqs = ROPE_THETA ** (-jnp.arange(AXIS_PAIRS, dtype=f32) / AXIS_PAIRS)
    ang = jnp.concatenate([row[:, None] * freqs, col[:, None] * freqs], axis=-1)
    cos, sin = jnp.cos(ang), jnp.sin(ang)
    cos_t = jnp.concatenate([cos, cos], axis=-1)
    sin_t = jnp.concatenate([-sin, sin], axis=-1)
    cos_t = jnp.concatenate([cos_t, jnp.ones((tile_rows, HEAD_DIM), f32)], axis=0)
    sin_t = jnp.concatenate([sin_t, jnp.zeros((tile_rows, HEAD_DIM), f32)], axis=0)
    return cos_t, sin_t


def _split_w_in(w):
    sizes = (ATT_HEADS * HEAD_DIM, KV_HEADS * HEAD_DIM, KV_HEADS * HEAD_DIM, GLA_KEY, GLA_KEY, GLA_VAL,
             GLA_VAL, GK_RANK, GK_RANK, D_MODEL, D_MODEL)
    offs = np.concatenate([[0], np.cumsum(sizes)])
    q, k, v, gq, gk, gv, gg, lrf, lrb, ga, gb = (w[:, offs[n]:offs[n + 1]] for n in range(len(sizes)))
    main = jnp.concatenate([q, gv, gg, ga, gb, gq, gk, k, v], axis=1).astype(bf16)
    lr = jnp.concatenate([lrf, lrb, jnp.zeros((D_MODEL, LANES - 2 * GK_RANK), w.dtype)], axis=1).astype(bf16)
    return main, lr


def kernel(x_prompt, x_sample, cache_k, cache_v, state_fwd, state_bwd, c, c_ctx, w_mod, b_mod, norm1, w_in,
           q_norm, k_norm, w_gk, b_gk, gla_norm, w_out, norm2, w_ff_gate, w_ff_up, w_ff_down, w_router,
           w_e_gate, w_e_up, w_e_down, final_norm):
    batch, seq, _ = x_prompt.shape
    dec_batch, dec_seq, _ = x_sample.shape
    depth = w_mod.shape[0]
    past = cache_k.shape[2]
    n_prompt = batch * seq
    n_sample = dec_batch * dec_seq
    t_all = n_prompt + n_sample
    assert dec_batch + 1 <= N_COND_PAD and n_prompt % dec_seq == 0 and seq % GLA_BLOCK == 0

    x_all = jnp.concatenate([x_prompt.reshape(n_prompt, D_MODEL), x_sample.reshape(n_sample, D_MODEL)], axis=0)
    cond = jnp.concatenate([c_ctx[None, :], c, jnp.zeros((N_COND_PAD - 1 - dec_batch, D_MODEL), f32)], axis=0)
    mod = _modulation(cond, w_mod, b_mod)

    proj_tm = _tile(min(n_prompt, dec_seq), 1024)
    cos_t, sin_t = _rope_tables(dec_seq, proj_tm)
    ck = cache_k.reshape(dec_batch, depth, past, KV_HEADS * HEAD_DIM)
    cv = cache_v.reshape(dec_batch, depth, past, KV_HEADS * HEAD_DIM)

    o_att = jnp.zeros((t_all, D_MODEL), bf16)
    o_f = jnp.zeros((t_all, GLA_VAL), bf16)
    o_b = jnp.zeros((t_all, GLA_VAL), bf16)
    new_k, new_v, new_sf, new_sb = [], [], [], []
    for l in range(depth):
        w_main, w_lr = _split_w_in(w_in[l])
        main, lr, kv32 = _in_proj(x_all, mod, l, norm1[l][None, :], w_main, w_lr, q_norm[l][None, :],
                                  k_norm[l][None, :], cos_t, sin_t, n_prompt, dec_seq)
        new_k.append(kv32[:n_prompt, :PROJ_TN].reshape(batch, seq, KV_HEADS, HEAD_DIM))
        new_v.append(kv32[:n_prompt, PROJ_TN:].reshape(batch, seq, KV_HEADS, HEAD_DIM))

        o_att = _attention(main, o_att, 0, batch, seq)
        o_att = _attention(main, o_att, n_prompt, dec_batch, dec_seq, ctx=(ck, cv, l))

        pad = jnp.zeros((LANES - GK_RANK, GLA_KEY), f32)
        wgk_f = jnp.concatenate([w_gk[l, 0], pad], axis=0).astype(bf16)
        wgk_b = jnp.concatenate([pad[:GK_RANK], w_gk[l, 1], pad[:LANES - 2 * GK_RANK]], axis=0).astype(bf16)
        bgk_f, bgk_b = b_gk[l, 0][None, :], b_gk[l, 1][None, :]
        o_f, o_b, s_f, s_b = _gla(main, lr, o_f, o_b, wgk_f, wgk_b, bgk_f, bgk_b, 0, batch, seq,
                                  emit_state=True)
        new_sf.append(s_f)
        new_sb.append(s_b)
        o_f, o_b = _gla(main, lr, o_f, o_b, wgk_f, wgk_b, bgk_f, bgk_b, n_prompt, dec_batch, dec_seq,
                        states=(state_fwd, state_bwd, l))

        x_all = _merge_out(x_all, mod, l, o_att, o_f, o_b, main, gla_norm[l][None, :], w_out[l].astype(bf16),
                           n_prompt, dec_seq)
        i = l // 2
        if l % 2 == 1:
            w_r = jnp.concatenate([w_router[i], jnp.zeros((D_MODEL, LANES - N_EXPERTS), f32)], axis=1)
            x_all = _moe_ffn(x_all, mod, l, norm2[l][None, :], w_r, w_e_gate[i].astype(bf16),
                             w_e_up[i].astype(bf16), w_e_down[i].astype(bf16), n_prompt, dec_seq)
        else:
            x_all = _dense_ffn(x_all, mod, l, norm2[l][None, :], w_ff_gate[i].astype(bf16),
                               w_ff_up[i].astype(bf16), w_ff_down[i].astype(bf16), n_prompt, dec_seq)

    y_prompt = _final_norm(x_all, final_norm[None, :], 0, n_prompt).reshape(batch, seq, D_MODEL)
    y_sample = _final_norm(x_all, final_norm[None, :], n_prompt, n_sample).reshape(dec_batch, dec_seq, D_MODEL)
    return (y_prompt, y_sample, jnp.stack(new_k, axis=1), jnp.stack(new_v, axis=1),
            jnp.stack(new_sf, axis=1), jnp.stack(new_sb, axis=1))
```

```python
import functools

import jax
import jax.numpy as jnp
import numpy as np
from jax import lax
from jax.experimental import pallas as pl
from jax.experimental.pallas import tpu as pltpu

D_MODEL = 2048
GRID_W = 64
ATT_HEADS = 16
KV_HEADS = 4
HEAD_DIM = 128
Q_GROUPS = ATT_HEADS // KV_HEADS
ROPE_THETA = 10000.0
AXIS_PAIRS = HEAD_DIM // 4
GLA_HEADS = 4
GLA_DK = D_MODEL // 8
GLA_DV = D_MODEL // 4
GLA_KEY = GLA_HEADS * GLA_DK
GLA_VAL = GLA_HEADS * GLA_DV
GK_RANK = 16
GK_NORMALIZER = 16.0
GLA_CHUNK = 64
FF_DENSE = ((8 * D_MODEL // 3 + 255) // 256) * 256
N_EXPERTS = 8
FF_EXPERT = 7 * D_MODEL // 2
EPS = 1e-6
LOG2_E = 1.4426950408889634

LANES = 128
N_COND_PAD = 16
VMEM_LIMIT = 56 * 1024 * 1024

COL_Q = 0
COL_GV = COL_Q + ATT_HEADS * HEAD_DIM
COL_GG = COL_GV + GLA_VAL
COL_GA = COL_GG + GLA_VAL
COL_GB = COL_GA + D_MODEL
COL_GQ = COL_GB + D_MODEL
COL_GK = COL_GQ + GLA_KEY
COL_K = COL_GK + GLA_KEY
COL_V = COL_K + KV_HEADS * HEAD_DIM
MAIN_COLS = COL_V + KV_HEADS * HEAD_DIM
PROJ_TN = Q_GROUPS * HEAD_DIM
PROJ_TILE = 2 * PROJ_TN
assert COL_V + PROJ_TN == MAIN_COLS and COL_K % PROJ_TILE == 0 and COL_GV % PROJ_TILE == 0
GLA_BLOCK = 4 * GLA_CHUNK
GLA_HEADS_PER_STEP = 4
DMA_UNROLL = 8

f32 = jnp.float32
bf16 = jnp.bfloat16
NT_DIMS = (((1,), (1,)), ((), ()))
TN_DIMS = (((0,), (0,)), ((), ()))


def _params(semantics):
    return pltpu.CompilerParams(dimension_semantics=semantics, vmem_limit_bytes=VMEM_LIMIT)


def _tile(n, pref):
    t = pref
    while n % t:
        t //= 2
    return t


def _cond_index(tile_rows, n_prompt, dec_seq):
    npt = n_prompt // tile_rows
    per = dec_seq // tile_rows
    return lambda i: jnp.where(i < npt, 0, 1 + (i - npt) // per)


def _silu(x):
    return x * jax.nn.sigmoid(x)


def _rms(x, gain):
    return x * lax.rsqrt(jnp.mean(x * x, axis=-1, keepdims=True) + EPS) * gain


def _mod_kernel(c_ref, w_ref, b_ref, o_ref):
    s = _silu(c_ref[...]).astype(bf16)
    o_ref[...] = jnp.dot(s, w_ref[...].astype(bf16), preferred_element_type=f32) + b_ref[...]


def _modulation(cond, w_mod, b_mod):
    depth = w_mod.shape[0]
    tn = 1024
    out = pl.pallas_call(
        _mod_kernel,
        grid=(depth, 6 * D_MODEL // tn),
        in_specs=[
            pl.BlockSpec((N_COND_PAD, D_MODEL), lambda l, n: (0, 0)),
            pl.BlockSpec((None, D_MODEL, tn), lambda l, n: (l, 0, n)),
            pl.BlockSpec((None, 1, tn), lambda l, n: (l, 0, n)),
        ],
        out_specs=pl.BlockSpec((None, N_COND_PAD, tn), lambda l, n: (l, 0, n)),
        out_shape=jax.ShapeDtypeStruct((depth, N_COND_PAD, 6 * D_MODEL), f32),
        compiler_params=_params(("arbitrary", "arbitrary")),
        name="modulation",
    )(cond, w_mod, b_mod.reshape(depth, 1, 6 * D_MODEL))
    return out.reshape(depth, N_COND_PAD, 6, D_MODEL)


def _in_proj_kernel(x_ref, mod_ref, n1_ref, w_ref, wlr_ref, qg_ref, kg_ref, cos_ref, sin_ref,
                    main_ref, lr_ref, kv_ref, h_ref):
    j = pl.program_id(1)
    jq = COL_GV // PROJ_TILE
    jkv = COL_K // PROJ_TILE
    kw = KV_HEADS * HEAD_DIM

    @pl.when(j == 0)
    def _():
        y = _rms(x_ref[...], n1_ref[...])
        hb = (y * (1.0 + mod_ref[1:2, :]) + mod_ref[0:1, :]).astype(bf16)
        h_ref[...] = hb
        lr_ref[...] = jnp.dot(hb, wlr_ref[...], preferred_element_type=f32)

    acc = jnp.dot(h_ref[...], w_ref[...], preferred_element_type=f32)

    def normed_and_rotated(sl, gain):
        n = _rms(acc[:, sl], gain)
        return n, n * cos_ref[...] + pltpu.roll(n, HEAD_DIM // 2, 1) * sin_ref[...]

    @pl.when(j < jq)
    def _():
        for hh in range(PROJ_TILE // HEAD_DIM):
            sl = slice(hh * HEAD_DIM, (hh + 1) * HEAD_DIM)
            _, r = normed_and_rotated(sl, qg_ref[...])
            main_ref[:, sl] = (r * (HEAD_DIM ** -0.5 * LOG2_E)).astype(bf16)

    @pl.when(j == jkv)
    def _():
        for hh in range(KV_HEADS):
            sl = slice(hh * HEAD_DIM, (hh + 1) * HEAD_DIM)
            n, r = normed_and_rotated(sl, kg_ref[...])
            kv_ref[:, sl] = n
            main_ref[:, sl] = r.astype(bf16)
        kv_ref[:, kw:] = acc[:, kw:]
        main_ref[:, kw:] = acc[:, kw:].astype(bf16)

    @pl.when((j >= jq) & (j != jkv))
    def _():
        main_ref[...] = acc.astype(bf16)


def _in_proj(x_all, mod, layer, norm1, w_main, w_lr, qg, kg, cos_t, sin_t, n_prompt, dec_seq):
    t_all = x_all.shape[0]
    tm = _tile(min(n_prompt, dec_seq), 1024)
    cidx = _cond_index(tm, n_prompt, dec_seq)
    npt = n_prompt // tm
    per = dec_seq // tm
    ridx = lambda i: jnp.where(i < npt, per, (i - npt) % per)
    return pl.pallas_call(
        _in_proj_kernel,
        grid=(t_all // tm, MAIN_COLS // PROJ_TILE),
        in_specs=[
            pl.BlockSpec((tm, D_MODEL), lambda i, j: (i, 0)),
            pl.BlockSpec((None, None, 6, D_MODEL), lambda i, j: (layer, cidx(i), 0, 0)),
            pl.BlockSpec((1, D_MODEL), lambda i, j: (0, 0)),
            pl.BlockSpec((D_MODEL, PROJ_TILE), lambda i, j: (0, j)),
            pl.BlockSpec((D_MODEL, LANES), lambda i, j: (0, 0)),
            pl.BlockSpec((1, HEAD_DIM), lambda i, j: (0, 0)),
            pl.BlockSpec((1, HEAD_DIM), lambda i, j: (0, 0)),
            pl.BlockSpec((tm, HEAD_DIM), lambda i, j: (ridx(i), 0)),
            pl.BlockSpec((tm, HEAD_DIM), lambda i, j: (ridx(i), 0)),
        ],
        out_specs=[
            pl.BlockSpec((tm, PROJ_TILE), lambda i, j: (i, j)),
            pl.BlockSpec((tm, LANES), lambda i, j: (i, 0)),
            pl.BlockSpec((tm, 2 * PROJ_TN), lambda i, j: (i, 0), pipeline_mode=pl.Buffered(1)),
        ],
        out_shape=[
            jax.ShapeDtypeStruct((t_all, MAIN_COLS), bf16),
            jax.ShapeDtypeStruct((t_all, LANES), f32),
            jax.ShapeDtypeStruct((t_all, 2 * PROJ_TN), f32),
        ],
        scratch_shapes=[pltpu.VMEM((tm, D_MODEL), bf16)],
        compiler_params=_params(("arbitrary", "arbitrary")),
        name="in_proj",
    )(x_all, mod, norm1, w_main, w_lr, qg, kg, cos_t, sin_t)


def _attn_kernel(*refs, has_ctx):
    if has_ctx:
        q_ref, k_ref, v_ref, ck_ref, cv_ref, _, o_ref, va_ref, ckb_ref, cva_ref = refs
    else:
        q_ref, k_ref, v_ref, _, o_ref, va_ref = refs

    @pl.when(pl.program_id(2) == 0)
    def _():
        va_ref[:, :HEAD_DIM] = v_ref[...]
        va_ref[:, HEAD_DIM:] = jnp.ones((va_ref.shape[0], HEAD_DIM), bf16)
        if has_ctx:
            ckb_ref[...] = ck_ref[...].astype(bf16)
            cva_ref[:, :HEAD_DIM] = cv_ref[...].astype(bf16)
            cva_ref[:, HEAD_DIM:] = jnp.ones((cva_ref.shape[0], HEAD_DIM), bf16)

    k = k_ref[...]

    def scores(g):
        q = q_ref[:, g * HEAD_DIM:(g + 1) * HEAD_DIM]
        s = lax.dot_general(q, k, NT_DIMS, preferred_element_type=f32)
        sc = lax.dot_general(q, ckb_ref[...], NT_DIMS, preferred_element_type=f32) if has_ctx else None
        return s, sc

    nxt = scores(0)
    for g in range(Q_GROUPS):
        s, sc = nxt
        if g + 1 < Q_GROUPS:
            nxt = scores(g + 1)
        m = jnp.max(s, axis=-1, keepdims=True)
        if has_ctx:
            m = jnp.maximum(m, jnp.max(sc, axis=-1, keepdims=True))
        oa = jnp.dot(jnp.exp2(s - m).astype(bf16), va_ref[...], preferred_element_type=f32)
        if has_ctx:
            oa = oa + jnp.dot(jnp.exp2(sc - m).astype(bf16), cva_ref[...], preferred_element_type=f32)
        o_ref[:, g * HEAD_DIM:(g + 1) * HEAD_DIM] = (oa[:, :HEAD_DIM] / oa[:, HEAD_DIM:]).astype(bf16)


def _attention(main, o_prev, row0, n_batch, seq, ctx=None):
    tq = _tile(seq, 512)
    nq = seq // tq
    assert row0 % seq == 0
    qb0 = row0 // tq
    kb0 = row0 // seq
    in_specs = [
        pl.BlockSpec((tq, PROJ_TN), lambda b, g, i: (qb0 + b * nq + i, g)),
        pl.BlockSpec((seq, HEAD_DIM), lambda b, g, i: (kb0 + b, COL_K // HEAD_DIM + g)),
        pl.BlockSpec((seq, HEAD_DIM), lambda b, g, i: (kb0 + b, COL_V // HEAD_DIM + g)),
    ]
    args = [main, main, main]
    if ctx is not None:
        ck, cv, layer = ctx
        past = ck.shape[2]
        spec = pl.BlockSpec((None, None, past, HEAD_DIM), lambda b, g, i: (b, layer, 0, g))
        in_specs += [spec, spec]
        args += [ck, cv]
    in_specs.append(pl.BlockSpec(memory_space=pl.ANY))
    args.append(o_prev)
    scratch = [pltpu.VMEM((seq, 2 * HEAD_DIM), bf16)]
    if ctx is not None:
        scratch += [pltpu.VMEM((past, HEAD_DIM), bf16), pltpu.VMEM((past, 2 * HEAD_DIM), bf16)]
    return pl.pallas_call(
        functools.partial(_attn_kernel, has_ctx=ctx is not None),
        grid=(n_batch, KV_HEADS, nq),
        in_specs=in_specs,
        out_specs=pl.BlockSpec((tq, PROJ_TN), lambda b, g, i: (qb0 + b * nq + i, g)),
        out_shape=jax.ShapeDtypeStruct(o_prev.shape, o_prev.dtype),
        scratch_shapes=scratch,
        input_output_aliases={len(args) - 1: 0},
        compiler_params=_params(("arbitrary", "arbitrary", "arbitrary")),
        name="attention_ctx" if ctx is not None else "attention",
    )(*args)


def _log_sigmoid(z):
    return jnp.minimum(z, 0.0) - jnp.log1p(jnp.exp(-jnp.abs(z)))


def _gla_chunk_prep(q, k, g, tri, mask, ref_row, last_row):
    g1 = g.astype(bf16)
    r1 = g - g1.astype(f32)
    g2 = r1.astype(bf16)
    g3 = (r1 - g2.astype(f32)).astype(bf16)
    b = (jnp.dot(tri, g1, preferred_element_type=f32) + jnp.dot(tri, g2, preferred_element_type=f32)
         + jnp.dot(tri, g3, preferred_element_type=f32))
    b_ref = b[ref_row:ref_row + 1, :]
    b_last = b[last_row:last_row + 1, :]
    qd = q * jnp.exp(b)
    kd = k * jnp.exp(b_last - b)
    qa = (q * jnp.exp(b - b_ref)).astype(bf16)
    ka = (k * jnp.exp(b_ref - b)).astype(bf16)
    a = lax.dot_general(qa, ka, NT_DIMS, preferred_element_type=f32)
    return qd, kd, jnp.where(mask, a, 0.0).astype(bf16), jnp.exp(b_last)


def _gla_block(q_ref, k_ref, v_ref, g, st_ref, o_ref, tri, mask, ref_row, last_row, order):
    n = len(order)
    q_scale = GLA_DK ** -0.5
    prep = {}
    for c in order:
        s = slice(c * GLA_CHUNK, (c + 1) * GLA_CHUNK)
        prep[c] = _gla_chunk_prep(q_ref[s, :].astype(f32) * q_scale, k_ref[s, :].astype(f32), g[s, :],
                                  tri, mask, ref_row, last_row)
    zero = jnp.zeros((GLA_CHUNK, GLA_CHUNK), bf16)
    blocks = [[zero] * n for _ in range(n)]
    q_rows = [None] * n
    k_rows = [None] * n
    into = None
    for pi, ci in enumerate(order):
        qd, _, a_diag, dec = prep[ci]
        blocks[ci][ci] = a_diag
        q_rows[ci] = (qd if into is None else qd * into).astype(bf16)
        between = None
        for pj in range(pi - 1, -1, -1):
            cj = order[pj]
            qs = (qd if between is None else qd * between).astype(bf16)
            blocks[ci][cj] = lax.dot_general(qs, prep[cj][1].astype(bf16), NT_DIMS,
                                             preferred_element_type=f32).astype(bf16)
            between = prep[cj][3] if between is None else between * prep[cj][3]
        into = dec if into is None else into * dec
    after = None
    for ci in reversed(order):
        kd = prep[ci][1]
        k_rows[ci] = (kd if after is None else kd * after).astype(bf16)
        after = prep[ci][3] if after is None else after * prep[ci][3]
    a_full = jnp.concatenate([jnp.concatenate(r, axis=1) for r in blocks], axis=0)
    q_blk = jnp.concatenate(q_rows, axis=0)
    k_blk = jnp.concatenate(k_rows, axis=0)
    v = v_ref[...]
    st = st_ref[...]
    o = (jnp.dot(a_full, v, preferred_element_type=f32)
         + lax.dot_general(q_blk, st.astype(bf16), NT_DIMS, preferred_element_type=f32))
    o_ref[...] = o.astype(o_ref.dtype)
    st_ref[...] = st * into + lax.dot_general(v, k_blk, TN_DIMS, preferred_element_type=f32)


def _gla_kernel(*refs, has_state, emit_state):
    (qf_ref, kf_ref, vf_ref, lf_ref, qb_ref, kb_ref, vb_ref, lb_ref,
     wf_ref, wb_ref, bf_ref, bb_ref) = refs[:12]
    refs = refs[12:]
    if has_state:
        s0f_ref, s0b_ref = refs[:2]
        refs = refs[2:]
    refs = refs[2:]
    of_ref, ob_ref = refs[:2]
    refs = refs[2:]
    if emit_state:
        sf_ref, sb_ref = refs[:2]
        refs = refs[2:]
    stf_ref, stb_ref = refs
    c = pl.program_id(2)
    heads = stf_ref.shape[0]

    @pl.when(c == 0)
    def _():
        for hh in range(heads):
            if has_state:
                stf_ref[hh] = s0f_ref[hh].T
                stb_ref[hh] = s0b_ref[hh].T
            else:
                stf_ref[hh] = jnp.zeros((GLA_DV, GLA_DK), f32)
                stb_ref[hh] = jnp.zeros((GLA_DV, GLA_DK), f32)

    row = lax.broadcasted_iota(jnp.int32, (GLA_CHUNK, GLA_CHUNK), 0)
    col = lax.broadcasted_iota(jnp.int32, (GLA_CHUNK, GLA_CHUNK), 1)
    lower = col <= row
    upper = col >= row
    tri_f = jnp.where(lower, 1.0, 0.0).astype(bf16)
    tri_b = jnp.where(upper, 1.0, 0.0).astype(bf16)
    mid = GLA_CHUNK // 2
    n_chunks = GLA_BLOCK // GLA_CHUNK

    def decay(l_ref, w_ref, b_ref):
        z = jnp.dot(l_ref[...].astype(bf16), w_ref[...], preferred_element_type=f32) + b_ref[...]
        return _log_sigmoid(z) * (1.0 / GK_NORMALIZER)

    g_f = decay(lf_ref, wf_ref, bf_ref)
    g_b = decay(lb_ref, wb_ref, bb_ref)
    order = list(range(n_chunks))
    for hh in range(heads):
        ks = slice(hh * GLA_DK, (hh + 1) * GLA_DK)
        vs = slice(hh * GLA_DV, (hh + 1) * GLA_DV)
        _gla_block(qf_ref.at[:, ks], kf_ref.at[:, ks], vf_ref.at[:, vs], g_f[:, ks], stf_ref.at[hh],
                   of_ref.at[:, vs], tri_f, lower, mid, GLA_CHUNK - 1, order)
        _gla_block(qb_ref.at[:, ks], kb_ref.at[:, ks], vb_ref.at[:, vs], g_b[:, ks], stb_ref.at[hh],
                   ob_ref.at[:, vs], tri_b, upper, GLA_CHUNK - 1 - mid, 0, order[::-1])

    if emit_state:
        @pl.when(c == pl.num_programs(2) - 1)
        def _():
            for hh in range(heads):
                sf_ref[hh] = stf_ref[hh].T
                sb_ref[hh] = stb_ref[hh].T


def _gla(main, lr, of_prev, ob_prev, wgk_f, wgk_b, bgk_f, bgk_b, row0, n_batch, seq,
         states=None, emit_state=False):
    nb = seq // GLA_BLOCK
    assert row0 % GLA_BLOCK == 0
    rb0 = row0 // GLA_BLOCK
    fwd = lambda b, c: rb0 + b * nb + c
    bwd = lambda b, c: rb0 + b * nb + (nb - 1 - c)

    hps = GLA_HEADS_PER_STEP
    dk, dv = hps * GLA_DK, hps * GLA_DV

    def token_specs(pos):
        return [
            pl.BlockSpec((GLA_BLOCK, dk), lambda b, h, c: (pos(b, c), COL_GQ // dk + h)),
            pl.BlockSpec((GLA_BLOCK, dk), lambda b, h, c: (pos(b, c), COL_GK // dk + h)),
            pl.BlockSpec((GLA_BLOCK, dv), lambda b, h, c: (pos(b, c), COL_GV // dv + h)),
            pl.BlockSpec((GLA_BLOCK, LANES), lambda b, h, c: (pos(b, c), 0)),
        ]

    in_specs = token_specs(fwd) + token_specs(bwd) + [
        pl.BlockSpec((LANES, dk), lambda b, h, c: (0, h)),
        pl.BlockSpec((LANES, dk), lambda b, h, c: (0, h)),
        pl.BlockSpec((1, dk), lambda b, h, c: (0, h)),
        pl.BlockSpec((1, dk), lambda b, h, c: (0, h)),
    ]
    args = [main, main, main, lr, main, main, main, lr, wgk_f, wgk_b, bgk_f, bgk_b]
    if states is not None:
        s_f, s_b, layer = states
        spec = pl.BlockSpec((None, None, hps, GLA_DK, GLA_DV), lambda b, h, c: (b, layer, h, 0, 0))
        in_specs += [spec, spec]
        args += [s_f, s_b]
    in_specs += [pl.BlockSpec(memory_space=pl.ANY)] * 2
    alias0 = len(args)
    args += [of_prev, ob_prev]
    out_specs = [
        pl.BlockSpec((GLA_BLOCK, dv), lambda b, h, c: (fwd(b, c), h)),
        pl.BlockSpec((GLA_BLOCK, dv), lambda b, h, c: (bwd(b, c), h)),
    ]
    out_shape = [jax.ShapeDtypeStruct(of_prev.shape, of_prev.dtype),
                 jax.ShapeDtypeStruct(ob_prev.shape, ob_prev.dtype)]
    if emit_state:
        spec = pl.BlockSpec((None, hps, GLA_DK, GLA_DV), lambda b, h, c: (b, h, 0, 0))
        out_specs += [spec, spec]
        out_shape += [jax.ShapeDtypeStruct((n_batch, GLA_HEADS, GLA_DK, GLA_DV), f32)] * 2
    return pl.pallas_call(
        functools.partial(_gla_kernel, has_state=states is not None, emit_state=emit_state),
        grid=(n_batch, GLA_HEADS // hps, nb),
        in_specs=in_specs,
        out_specs=out_specs,
        out_shape=out_shape,
        scratch_shapes=[pltpu.VMEM((hps, GLA_DV, GLA_DK), f32), pltpu.VMEM((hps, GLA_DV, GLA_DK), f32)],
        input_output_aliases={alias0: 0, alias0 + 1: 1},
        compiler_params=_params(("arbitrary", "arbitrary", "arbitrary")),
        name="gla_state" if states is not None else "gla",
    )(*args)


def _merge_kernel(x_ref, mod_ref, oa_ref, of_ref, ob_ref, gg_ref, ga_ref, gb_ref, gn_ref, w_ref,
                  o_ref, m_ref):
    for hh in range(GLA_HEADS):
        sl = slice(hh * GLA_DV, (hh + 1) * GLA_DV)
        og = of_ref[:, sl].astype(f32) + ob_ref[:, sl].astype(f32)
        gla = _rms(og, gn_ref[...]) * _silu(gg_ref[:, sl].astype(f32))
        merged = (jax.nn.sigmoid(ga_ref[:, sl].astype(f32)) * oa_ref[:, sl].astype(f32)
                  + jax.nn.sigmoid(gb_ref[:, sl].astype(f32)) * gla)
        m_ref[:, sl] = merged.astype(bf16)
    mix = jnp.dot(m_ref[...], w_ref[...], preferred_element_type=f32)
    o_ref[...] = x_ref[...] + mod_ref[2:3, :] * mix


def _merge_out(x_all, mod, layer, o_att, o_f, o_b, main, gla_norm, w_out, n_prompt, dec_seq):
    t_all = x_all.shape[0]
    tm = _tile(min(n_prompt, dec_seq), 256)
    cidx = _cond_index(tm, n_prompt, dec_seq)
    row = lambda i: (i, 0)
    return pl.pallas_call(
        _merge_kernel,
        grid=(t_all // tm,),
        in_specs=[
            pl.BlockSpec((tm, D_MODEL), row),
            pl.BlockSpec((None, None, 6, D_MODEL), lambda i: (layer, cidx(i), 0, 0)),
            pl.BlockSpec((tm, D_MODEL), row),
            pl.BlockSpec((tm, GLA_VAL), row),
            pl.BlockSpec((tm, GLA_VAL), row),
            pl.BlockSpec((tm, GLA_VAL), lambda i: (i, COL_GG // GLA_VAL)),
            pl.BlockSpec((tm, D_MODEL), lambda i: (i, COL_GA // D_MODEL)),
            pl.BlockSpec((tm, D_MODEL), lambda i: (i, COL_GB // D_MODEL)),
            pl.BlockSpec((1, GLA_DV), lambda i: (0, 0)),
            pl.BlockSpec((D_MODEL, D_MODEL), lambda i: (0, 0)),
        ],
        out_specs=pl.BlockSpec((tm, D_MODEL), row),
        out_shape=jax.ShapeDtypeStruct(x_all.shape, f32),
        scratch_shapes=[pltpu.VMEM((tm, D_MODEL), bf16)],
        compiler_params=_params(("arbitrary",)),
        name="merge_out",
    )(x_all, mod, o_att, o_f, o_b, main, main, main, gla_norm, w_out)


def _ffn_kernel(x_ref, mod_ref, n2_ref, wg_ref, wu_ref, wd_ref, o_ref, h_ref, acc_ref):
    j = pl.program_id(1)

    @pl.when(j == 0)
    def _():
        y = _rms(x_ref[...], n2_ref[...])
        h_ref[...] = (y * (1.0 + mod_ref[4:5, :]) + mod_ref[3:4, :]).astype(bf16)
        acc_ref[...] = jnp.zeros_like(acc_ref)

    h = h_ref[...]
    gate = jnp.dot(h, wg_ref[...], preferred_element_type=f32)
    up = jnp.dot(h, wu_ref[...], preferred_element_type=f32)
    acc_ref[...] += jnp.dot((_silu(gate) * up).astype(bf16), wd_ref[...], preferred_element_type=f32)

    @pl.when(j == pl.num_programs(1) - 1)
    def _():
        o_ref[...] = x_ref[...] + mod_ref[5:6, :] * acc_ref[...]


def _dense_ffn(x_all, mod, layer, norm2, wg, wu, wd, n_prompt, dec_seq):
    t_all = x_all.shape[0]
    tm = _tile(min(n_prompt, dec_seq), 512)
    tf = 512
    cidx = _cond_index(tm, n_prompt, dec_seq)
    return pl.pallas_call(
        _ffn_kernel,
        grid=(t_all // tm, FF_DENSE // tf),
        in_specs=[
            pl.BlockSpec((tm, D_MODEL), lambda i, j: (i, 0)),
            pl.BlockSpec((None, None, 6, D_MODEL), lambda i, j: (layer, cidx(i), 0, 0)),
            pl.BlockSpec((1, D_MODEL), lambda i, j: (0, 0)),
            pl.BlockSpec((D_MODEL, tf), lambda i, j: (0, j)),
            pl.BlockSpec((D_MODEL, tf), lambda i, j: (0, j)),
            pl.BlockSpec((tf, D_MODEL), lambda i, j: (j, 0)),
        ],
        out_specs=pl.BlockSpec((tm, D_MODEL), lambda i, j: (i, 0)),
        out_shape=jax.ShapeDtypeStruct(x_all.shape, f32),
        scratch_shapes=[pltpu.VMEM((tm, D_MODEL), bf16), pltpu.VMEM((tm, D_MODEL), f32)],
        compiler_params=_params(("arbitrary", "arbitrary")),
        name="dense_ffn",
    )(x_all, mod, norm2, wg, wu, wd)


INFO_E1, INFO_E2, INFO_W1, INFO_W2, INFO_R1, INFO_R2 = range(6)


def _route_kernel(x_ref, mod_ref, n2_ref, wr_ref, h_ref, info_ref, cnt_ref, run_ref):
    i = pl.program_id(0)
    tm = x_ref.shape[0]

    @pl.when(i == 0)
    def _():
        run_ref[...] = jnp.zeros_like(run_ref)

    y = _rms(x_ref[...], n2_ref[...])
    h = y * (1.0 + mod_ref[4:5, :]) + mod_ref[3:4, :]
    h_ref[...] = h
    logits = jnp.dot(h, wr_ref[...], preferred_element_type=f32, precision=lax.Precision.HIGHEST)
    lane = lax.broadcasted_iota(jnp.int32, (tm, LANES), 1)
    lane_f = lane.astype(f32)
    neg = -jnp.inf
    lg = jnp.where(lane < N_EXPERTS, logits, neg)
    m1 = jnp.max(lg, axis=-1, keepdims=True)
    i1 = jnp.min(jnp.where(lg == m1, lane_f, float(LANES)), axis=-1, keepdims=True)
    e1 = lane_f == i1
    lg2 = jnp.where(e1, neg, lg)
    m2 = jnp.max(lg2, axis=-1, keepdims=True)
    i2 = jnp.min(jnp.where(lg2 == m2, lane_f, float(LANES)), axis=-1, keepdims=True)
    e2 = lane_f == i2
    ex = jnp.exp(m2 - m1)
    w1 = 1.0 / (1.0 + ex)
    w2 = ex / (1.0 + ex)
    e1f = jnp.where(e1, 1.0, 0.0)
    e2f = jnp.where(e2, 1.0, 0.0)
    both = e1f + e2f
    row = lax.broadcasted_iota(jnp.int32, (tm, tm), 0)
    col = lax.broadcasted_iota(jnp.int32, (tm, tm), 1)
    strict = jnp.where(col < row, 1.0, 0.0).astype(bf16)
    before = jnp.dot(strict, both.astype(bf16), preferred_element_type=f32) + run_ref[...]
    r1 = jnp.sum(before * e1f, axis=-1, keepdims=True)
    r2 = jnp.sum(before * e2f, axis=-1, keepdims=True)
    run = run_ref[...] + jnp.sum(both, axis=0, keepdims=True)
    run_ref[...] = run
    cnt_ref[...] = run
    info = jnp.zeros((tm, LANES), f32)
    for idx, val in ((INFO_E1, i1), (INFO_E2, i2), (INFO_W1, w1), (INFO_W2, w2),
                     (INFO_R1, r1), (INFO_R2, r2)):
        info = jnp.where(lane == idx, val, info)
    info_ref[...] = info


def _route(x_all, mod, layer, norm2, w_router, n_prompt, dec_seq):
    t_all = x_all.shape[0]
    tm = _tile(min(n_prompt, dec_seq), 512)
    cidx = _cond_index(tm, n_prompt, dec_seq)
    return pl.pallas_call(
        _route_kernel,
        grid=(t_all // tm,),
        in_specs=[
            pl.BlockSpec((tm, D_MODEL), lambda i: (i, 0)),
            pl.BlockSpec((None, None, 6, D_MODEL), lambda i: (layer, cidx(i), 0, 0)),
            pl.BlockSpec((1, D_MODEL), lambda i: (0, 0)),
            pl.BlockSpec((D_MODEL, LANES), lambda i: (0, 0)),
        ],
        out_specs=[
            pl.BlockSpec((tm, D_MODEL), lambda i: (i, 0)),
            pl.BlockSpec((tm, LANES), lambda i: (i, 0)),
            pl.BlockSpec((1, LANES), lambda i: (0, 0)),
        ],
        out_shape=[
            jax.ShapeDtypeStruct((t_all, D_MODEL), f32),
            jax.ShapeDtypeStruct((t_all, LANES), f32),
            jax.ShapeDtypeStruct((1, LANES), f32),
        ],
        scratch_shapes=[pltpu.VMEM((1, LANES), f32)],
        compiler_params=_params(("arbitrary",)),
        name="route",
    )(x_all, mod, norm2, w_router)


def _row_copy(src, src_row, dst, dst_row, sem):
    return pltpu.make_async_copy(src.at[pl.ds(src_row, 1)], dst.at[pl.ds(dst_row, 1)], sem)


def _dispatch_kernel(tok_ref, h_hbm, o_ref, buf_ref, sem, *, rows):
    base = pl.program_id(0) * rows

    def issue(r, carry):
        _row_copy(h_hbm, tok_ref[base + r], buf_ref, r, sem).start()
        return carry

    lax.fori_loop(0, rows, issue, 0, unroll=DMA_UNROLL)

    def drain(r, carry):
        _row_copy(h_hbm, 0, buf_ref, 0, sem).wait()
        return carry

    lax.fori_loop(0, rows, drain, 0, unroll=DMA_UNROLL)
    o_ref[...] = buf_ref[...].astype(bf16)


def _dispatch(slot_token, h_all):
    n_rows = slot_token.shape[0]
    rows = _tile(n_rows, 256)
    return pl.pallas_call(
        functools.partial(_dispatch_kernel, rows=rows),
        grid_spec=pltpu.PrefetchScalarGridSpec(
            num_scalar_prefetch=1,
            grid=(n_rows // rows,),
            in_specs=[pl.BlockSpec(memory_space=pl.ANY)],
            out_specs=pl.BlockSpec((rows, D_MODEL), lambda i, t: (i, 0)),
            scratch_shapes=[pltpu.VMEM((rows, D_MODEL), f32), pltpu.SemaphoreType.DMA(())],
        ),
        out_shape=jax.ShapeDtypeStruct((n_rows, D_MODEL), bf16),
        compiler_params=_params(("arbitrary",)),
        name="moe_dispatch",
    )(slot_token, h_all)


def _expert_kernel(be_ref, na_ref, x_ref, wg_ref, wu_ref, wd_ref, o_ref, acc_ref):
    b = pl.program_id(0)
    j = pl.program_id(1)
    active = b < na_ref[0]

    @pl.when(active & (j == 0))
    def _():
        acc_ref[...] = jnp.zeros_like(acc_ref)

    @pl.when(active)
    def _():
        h = x_ref[...]
        gate = jnp.dot(h, wg_ref[...], preferred_element_type=f32)
        up = jnp.dot(h, wu_ref[...], preferred_element_type=f32)
        acc_ref[...] += jnp.dot((_silu(gate) * up).astype(bf16), wd_ref[...], preferred_element_type=f32)

    @pl.when(j == pl.num_programs(1) - 1)
    def _():
        o_ref[...] = jnp.where(active, acc_ref[...], 0.0)


def _expert_ffn(block_expert, n_active, sorted_h, wg, wu, wd, tmb):
    n_rows = sorted_h.shape[0]
    tf = 1024
    nf = FF_EXPERT // tf

    def ff(b, j, na):
        return jnp.where(b < na[0], j, nf - 1)

    return pl.pallas_call(
        _expert_kernel,
        grid_spec=pltpu.PrefetchScalarGridSpec(
            num_scalar_prefetch=2,
            grid=(n_rows // tmb, nf),
            in_specs=[
                pl.BlockSpec((tmb, D_MODEL), lambda b, j, be, na: (b, 0)),
                pl.BlockSpec((None, D_MODEL, tf), lambda b, j, be, na: (be[b], 0, ff(b, j, na))),
                pl.BlockSpec((None, D_MODEL, tf), lambda b, j, be, na: (be[b], 0, ff(b, j, na))),
                pl.BlockSpec((None, tf, D_MODEL), lambda b, j, be, na: (be[b], ff(b, j, na), 0)),
            ],
            out_specs=pl.BlockSpec((tmb, D_MODEL), lambda b, j, be, na: (b, 0)),
            scratch_shapes=[pltpu.VMEM((tmb, D_MODEL), f32)],
        ),
        out_shape=jax.ShapeDtypeStruct((n_rows, D_MODEL), f32),
        compiler_params=_params(("arbitrary", "arbitrary")),
        name="expert_ffn",
    )(block_expert, n_active, sorted_h, wg, wu, wd)


def _combine_kernel(dest_ref, x_ref, mod_ref, info_ref, y_hbm, o_ref, buf_ref, sem, *, rows):
    base = pl.program_id(0) * rows

    def issue(r, carry):
        for kk in range(2):
            _row_copy(y_hbm, dest_ref[2 * (base + r) + kk], buf_ref.at[kk], r, sem).start()
        return carry

    lax.fori_loop(0, rows, issue, 0, unroll=DMA_UNROLL)

    def drain(r, carry):
        for kk in range(2):
            _row_copy(y_hbm, 0, buf_ref.at[kk], 0, sem).wait()
        return carry

    lax.fori_loop(0, rows, drain, 0, unroll=DMA_UNROLL)
    info = info_ref[...]
    w1 = info[:, INFO_W1:INFO_W1 + 1]
    w2 = info[:, INFO_W2:INFO_W2 + 1]
    o_ref[...] = x_ref[...] + mod_ref[5:6, :] * (w1 * buf_ref[0] + w2 * buf_ref[1])


def _combine(dest, x_all, mod, layer, info, y_sorted, n_prompt, dec_seq):
    t_all = x_all.shape[0]
    rows = _tile(min(n_prompt, dec_seq), 256)
    cidx = _cond_index(rows, n_prompt, dec_seq)
    return pl.pallas_call(
        functools.partial(_combine_kernel, rows=rows),
        grid_spec=pltpu.PrefetchScalarGridSpec(
            num_scalar_prefetch=1,
            grid=(t_all // rows,),
            in_specs=[
                pl.BlockSpec((rows, D_MODEL), lambda i, d: (i, 0)),
                pl.BlockSpec((None, None, 6, D_MODEL), lambda i, d: (layer, cidx(i), 0, 0)),
                pl.BlockSpec((rows, LANES), lambda i, d: (i, 0)),
                pl.BlockSpec(memory_space=pl.ANY),
            ],
            out_specs=pl.BlockSpec((rows, D_MODEL), lambda i, d: (i, 0)),
            scratch_shapes=[pltpu.VMEM((2, rows, D_MODEL), f32), pltpu.SemaphoreType.DMA(())],
        ),
        out_shape=jax.ShapeDtypeStruct(x_all.shape, f32),
        compiler_params=_params(("arbitrary",)),
        name="moe_combine",
    )(dest, x_all, mod, info, y_sorted)


def _moe_ffn(x_all, mod, layer, norm2, w_router, wg, wu, wd, n_prompt, dec_seq):
    t_all = x_all.shape[0]
    tmb = _tile(t_all, 512)
    n_blocks = (2 * t_all) // tmb + N_EXPERTS
    h_all, info, counts = _route(x_all, mod, layer, norm2, w_router, n_prompt, dec_seq)
    counts = counts[0, :N_EXPERTS].astype(jnp.int32)
    padded = ((counts + tmb - 1) // tmb) * tmb
    end = jnp.cumsum(padded)
    start = end - padded
    e12 = info[:, INFO_E1:INFO_E2 + 1].astype(jnp.int32)
    r12 = info[:, INFO_R1:INFO_R2 + 1].astype(jnp.int32)
    dest = (start[e12] + r12).reshape(2 * t_all)
    slot_token = jnp.zeros((n_blocks * tmb,), jnp.int32).at[dest].set(
        jnp.arange(2 * t_all, dtype=jnp.int32) // 2, unique_indices=True)
    n_active = (end[-1] // tmb).astype(jnp.int32).reshape(1)
    blk = jnp.minimum(jnp.arange(n_blocks, dtype=jnp.int32), n_active[0] - 1) * tmb
    block_expert = jnp.minimum(jnp.sum(blk[:, None] >= end[None, :], axis=1), N_EXPERTS - 1).astype(jnp.int32)
    sorted_h = _dispatch(slot_token, h_all)
    y_sorted = _expert_ffn(block_expert, n_active, sorted_h, wg, wu, wd, tmb)
    return _combine(dest, x_all, mod, layer, info, y_sorted, n_prompt, dec_seq)


def _final_kernel(x_ref, g_ref, o_ref):
    o_ref[...] = _rms(x_ref[...], g_ref[...])


def _final_norm(x_all, gain, row0, n_rows):
    tm = _tile(n_rows, 512)
    assert row0 % tm == 0
    return pl.pallas_call(
        _final_kernel,
        grid=(n_rows // tm,),
        in_specs=[pl.BlockSpec((tm, D_MODEL), lambda i: (row0 // tm + i, 0)),
                  pl.BlockSpec((1, D_MODEL), lambda i: (0, 0))],
        out_specs=pl.BlockSpec((tm, D_MODEL), lambda i: (i, 0)),
        out_shape=jax.ShapeDtypeStruct((n_rows, D_MODEL), f32),
        compiler_params=_params(("arbitrary",)),
        name="final_norm",
    )(x_all, gain)


def _rope_tables(dec_seq, tile_rows):
    rows = dec_seq // GRID_W
    row = jnp.repeat(jnp.arange(rows, dtype=f32), GRID_W)
    col = jnp.tile(jnp.arange(GRID_W, dtype=f32), rows)
    freqs = ROPE_THETA ** (-jnp.arange(AXIS_PAIRS, dtype=f32) / AXIS_PAIRS)
    ang = jnp.concatenate([row[:, None] * freqs, col[:, None] * freqs], axis=-1)
    cos, sin = jnp.cos(ang), jnp.sin(ang)
    cos_t = jnp.concatenate([cos, cos], axis=-1)
    sin_t = jnp.concatenate([-sin, sin], axis=-1)
    cos_t = jnp.concatenate([cos_t, jnp.ones((tile_rows, HEAD_DIM), f32)], axis=0)
    sin_t = jnp.concatenate([sin_t, jnp.zeros((tile_rows, HEAD_DIM), f32)], axis=0)
    return cos_t, sin_t


def _split_w_in(w):
    sizes = (ATT_HEADS * HEAD_DIM, KV_HEADS * HEAD_DIM, KV_HEADS * HEAD_DIM, GLA_KEY, GLA_KEY, GLA_VAL,
             GLA_VAL, GK_RANK, GK_RANK, D_MODEL, D_MODEL)
    offs = np.concatenate([[0], np.cumsum(sizes)])
    q, k, v, gq, gk, gv, gg, lrf, lrb, ga, gb = (w[:, offs[n]:offs[n + 1]] for n in range(len(sizes)))
    main = jnp.concatenate([q, gv, gg, ga, gb, gq, gk, k, v], axis=1).astype(bf16)
    lr = jnp.concatenate([lrf, lrb, jnp.zeros((D_MODEL, LANES - 2 * GK_RANK), w.dtype)], axis=1).astype(bf16)
    return main, lr


def kernel(x_prompt, x_sample, cache_k, cache_v, state_fwd, state_bwd, c, c_ctx, w_mod, b_mod, norm1, w_in,
           q_norm, k_norm, w_gk, b_gk, gla_norm, w_out, norm2, w_ff_gate, w_ff_up, w_ff_down, w_router,
           w_e_gate, w_e_up, w_e_down, final_norm):
    batch, seq, _ = x_prompt.shape
    dec_batch, dec_seq, _ = x_sample.shape
    depth = w_mod.shape[0]
    past = cache_k.shape[2]
    n_prompt = batch * seq
    n_sample = dec_batch * dec_seq
    t_all = n_prompt + n_sample
    assert dec_batch + 1 <= N_COND_PAD and n_prompt % dec_seq == 0 and seq % GLA_BLOCK == 0

    x_all = jnp.concatenate([x_prompt.reshape(n_prompt, D_MODEL), x_sample.reshape(n_sample, D_MODEL)], axis=0)
    cond = jnp.concatenate([c_ctx[None, :], c, jnp.zeros((N_COND_PAD - 1 - dec_batch, D_MODEL), f32)], axis=0)
    mod = _modulation(cond, w_mod, b_mod)

    proj_tm = _tile(min(n_prompt, dec_seq), 1024)
    cos_t, sin_t = _rope_tables(dec_seq, proj_tm)
    ck = cache_k.reshape(dec_batch, depth, past, KV_HEADS * HEAD_DIM)
    cv = cache_v.reshape(dec_batch, depth, past, KV_HEADS * HEAD_DIM)

    o_att = jnp.zeros((t_all, D_MODEL), bf16)
    o_f = jnp.zeros((t_all, GLA_VAL), bf16)
    o_b = jnp.zeros((t_all, GLA_VAL), bf16)
    new_k, new_v, new_sf, new_sb = [], [], [], []
    for l in range(depth):
        w_main, w_lr = _split_w_in(w_in[l])
        main, lr, kv32 = _in_proj(x_all, mod, l, norm1[l][None, :], w_main, w_lr, q_norm[l][None, :],
                                  k_norm[l][None, :], cos_t, sin_t, n_prompt, dec_seq)
        new_k.append(kv32[:n_prompt, :PROJ_TN].reshape(batch, seq, KV_HEADS, HEAD_DIM))
        new_v.append(kv32[:n_prompt, PROJ_TN:].reshape(batch, seq, KV_HEADS, HEAD_DIM))

        o_att = _attention(main, o_att, 0, batch, seq)
        o_att = _attention(main, o_att, n_prompt, dec_batch, dec_seq, ctx=(ck, cv, l))

        pad = jnp.zeros((LANES - GK_RANK, GLA_KEY), f32)
        wgk_f = jnp.concatenate([w_gk[l, 0], pad], axis=0).astype(bf16)
        wgk_b = jnp.concatenate([pad[:GK_RANK], w_gk[l, 1], pad[:LANES - 2 * GK_RANK]], axis=0).astype(bf16)
        bgk_f, bgk_b = b_gk[l, 0][None, :], b_gk[l, 1][None, :]
        o_f, o_b, s_f, s_b = _gla(main, lr, o_f, o_b, wgk_f, wgk_b, bgk_f, bgk_b, 0, batch, seq,
                                  emit_state=True)
        new_sf.append(s_f)
        new_sb.append(s_b)
        o_f, o_b = _gla(main, lr, o_f, o_b, wgk_f, wgk_b, bgk_f, bgk_b, n_prompt, dec_batch, dec_seq,
                        states=(state_fwd, state_bwd, l))

        x_all = _merge_out(x_all, mod, l, o_att, o_f, o_b, main, gla_norm[l][None, :], w_out[l].astype(bf16),
                           n_prompt, dec_seq)
        i = l // 2
        if l % 2 == 1:
            w_r = jnp.concatenate([w_router[i], jnp.zeros((D_MODEL, LANES - N_EXPERTS), f32)], axis=1)
            x_all = _moe_ffn(x_all, mod, l, norm2[l][None, :], w_r, w_e_gate[i].astype(bf16),
                             w_e_up[i].astype(bf16), w_e_down[i].astype(bf16), n_prompt, dec_seq)
        else:
            x_all = _dense_ffn(x_all, mod, l, norm2[l][None, :], w_ff_gate[i].astype(bf16),
                               w_ff_up[i].astype(bf16), w_ff_down[i].astype(bf16), n_prompt, dec_seq)

    y_prompt = _final_norm(x_all, final_norm[None, :], 0, n_prompt).reshape(batch, seq, D_MODEL)
    y_sample = _final_norm(x_all, final_norm[None, :], n_prompt, n_sample).reshape(dec_batch, dec_seq, D_MODEL)
    return (y_prompt, y_sample, jnp.stack(new_k, axis=1), jnp.stack(new_v, axis=1),
            jnp.stack(new_sf, axis=1), jnp.stack(new_sb, axis=1))
```

```python
import functools

import jax
import jax.numpy as jnp
import numpy as np
from jax import lax
from jax.experimental import pallas as pl
from jax.experimental.pallas import tpu as pltpu

D_MODEL = 2048
GRID_W = 64
ATT_HEADS = 16
KV_HEADS = 4
HEAD_DIM = 128
Q_GROUPS = ATT_HEADS // KV_HEADS
ROPE_THETA = 10000.0
AXIS_PAIRS = HEAD_DIM // 4
GLA_HEADS = 4
GLA_DK = D_MODEL // 8
GLA_DV = D_MODEL // 4
GLA_KEY = GLA_HEADS * GLA_DK
GLA_VAL = GLA_HEADS * GLA_DV
GK_RANK = 16
GK_NORMALIZER = 16.0
GLA_CHUNK = 64
FF_DENSE = ((8 * D_MODEL // 3 + 255) // 256) * 256
N_EXPERTS = 8
FF_EXPERT = 7 * D_MODEL // 2
EPS = 1e-6
LOG2_E = 1.4426950408889634

LANES = 128
N_COND_PAD = 16
VMEM_LIMIT = 56 * 1024 * 1024

COL_Q = 0
COL_GV = COL_Q + ATT_HEADS * HEAD_DIM
COL_GG = COL_GV + GLA_VAL
COL_GA = COL_GG + GLA_VAL
COL_GB = COL_GA + D_MODEL
COL_GQ = COL_GB + D_MODEL
COL_GK = COL_GQ + GLA_KEY
COL_K = COL_GK + GLA_KEY
COL_V = COL_K + KV_HEADS * HEAD_DIM
MAIN_COLS = COL_V + KV_HEADS * HEAD_DIM
PROJ_TN = Q_GROUPS * HEAD_DIM
PROJ_TILE = 2 * PROJ_TN
assert COL_V + PROJ_TN == MAIN_COLS and COL_K % PROJ_TILE == 0 and COL_GV % PROJ_TILE == 0
GLA_BLOCK = 4 * GLA_CHUNK
GLA_HEADS_PER_STEP = 4
PROJ_ROW_CHUNK = 256
DMA_UNROLL = 8

f32 = jnp.float32
bf16 = jnp.bfloat16
NT_DIMS = (((1,), (1,)), ((), ()))
TN_DIMS = (((0,), (0,)), ((), ()))


def _params(semantics):
    return pltpu.CompilerParams(dimension_semantics=semantics, vmem_limit_bytes=VMEM_LIMIT)


def _tile(n, pref):
    t = pref
    while n % t:
        t //= 2
    return t


def _cond_index(tile_rows, n_prompt, dec_seq):
    npt = n_prompt // tile_rows
    per = dec_seq // tile_rows
    return lambda i: jnp.where(i < npt, 0, 1 + (i - npt) // per)


def _silu(x):
    return x * jax.nn.sigmoid(x)


def _rms(x, gain):
    return x * lax.rsqrt(jnp.mean(x * x, axis=-1, keepdims=True) + EPS) * gain


def _mod_kernel(c_ref, w_ref, b_ref, o_ref):
    s = _silu(c_ref[...]).astype(bf16)
    o_ref[...] = jnp.dot(s, w_ref[...].astype(bf16), preferred_element_type=f32) + b_ref[...]


def _modulation(cond, w_mod, b_mod):
    depth = w_mod.shape[0]
    tn = 1024
    out = pl.pallas_call(
        _mod_kernel,
        grid=(depth, 6 * D_MODEL // tn),
        in_specs=[
            pl.BlockSpec((N_COND_PAD, D_MODEL), lambda l, n: (0, 0)),
            pl.BlockSpec((None, D_MODEL, tn), lambda l, n: (l, 0, n)),
            pl.BlockSpec((None, 1, tn), lambda l, n: (l, 0, n)),
        ],
        out_specs=pl.BlockSpec((None, N_COND_PAD, tn), lambda l, n: (l, 0, n)),
        out_shape=jax.ShapeDtypeStruct((depth, N_COND_PAD, 6 * D_MODEL), f32),
        compiler_params=_params(("arbitrary", "arbitrary")),
        name="modulation",
    )(cond, w_mod, b_mod.reshape(depth, 1, 6 * D_MODEL))
    return out.reshape(depth, N_COND_PAD, 6, D_MODEL)


def _in_proj_kernel(x_ref, mod_ref, n1_ref, w_ref, wlr_ref, qg_ref, kg_ref, cos_ref, sin_ref,
                    main_ref, lr_ref, kv_ref, h_ref):
    j = pl.program_id(1)
    jq = COL_GV // PROJ_TILE
    jkv = COL_K // PROJ_TILE
    kw = KV_HEADS * HEAD_DIM

    @pl.when(j == 0)
    def _():
        y = _rms(x_ref[...], n1_ref[...])
        hb = (y * (1.0 + mod_ref[1:2, :]) + mod_ref[0:1, :]).astype(bf16)
        h_ref[...] = hb
        lr_ref[...] = jnp.dot(hb, wlr_ref[...], preferred_element_type=f32)

    tm = h_ref.shape[0]
    n_row_chunks = max(1, tm // PROJ_ROW_CHUNK)
    rc = tm // n_row_chunks

    def run(epilogue):
        chunk = lambda r: jnp.dot(h_ref[r * rc:(r + 1) * rc, :], w_ref[...], preferred_element_type=f32)
        nxt = chunk(0)
        for r in range(n_row_chunks):
            acc = nxt
            if r + 1 < n_row_chunks:
                nxt = chunk(r + 1)
            epilogue(slice(r * rc, (r + 1) * rc), acc)

    def normed_and_rotated(rows, acc, sl, gain):
        n = _rms(acc[:, sl], gain)
        return n, n * cos_ref[rows, :] + pltpu.roll(n, HEAD_DIM // 2, 1) * sin_ref[rows, :]

    def q_epilogue(rows, acc):
        for hh in range(PROJ_TILE // HEAD_DIM):
            sl = slice(hh * HEAD_DIM, (hh + 1) * HEAD_DIM)
            _, r = normed_and_rotated(rows, acc, sl, qg_ref[...])
            main_ref[rows, sl] = (r * (HEAD_DIM ** -0.5 * LOG2_E)).astype(bf16)

    def kv_epilogue(rows, acc):
        for hh in range(KV_HEADS):
            sl = slice(hh * HEAD_DIM, (hh + 1) * HEAD_DIM)
            n, r = normed_and_rotated(rows, acc, sl, kg_ref[...])
            kv_ref[rows, sl] = n
            main_ref[rows, sl] = r.astype(bf16)
        kv_ref[rows, kw:] = acc[:, kw:]
        main_ref[rows, kw:] = acc[:, kw:].astype(bf16)

    def plain_epilogue(rows, acc):
        main_ref[rows, :] = acc.astype(bf16)

    pl.when(j < jq)(lambda: run(q_epilogue))
    pl.when(j == jkv)(lambda: run(kv_epilogue))
    pl.when((j >= jq) & (j != jkv))(lambda: run(plain_epilogue))


def _in_proj(x_all, mod, layer, norm1, w_main, w_lr, qg, kg, cos_t, sin_t, n_prompt, dec_seq):
    t_all = x_all.shape[0]
    tm = _tile(min(n_prompt, dec_seq), 1024)
    cidx = _cond_index(tm, n_prompt, dec_seq)
    npt = n_prompt // tm
    per = dec_seq // tm
    ridx = lambda i: jnp.where(i < npt, per, (i - npt) % per)
    return pl.pallas_call(
        _in_proj_kernel,
        grid=(t_all // tm, MAIN_COLS // PROJ_TILE),
        in_specs=[
            pl.BlockSpec((tm, D_MODEL), lambda i, j: (i, 0)),
            pl.BlockSpec((None, None, 6, D_MODEL), lambda i, j: (layer, cidx(i), 0, 0)),
            pl.BlockSpec((1, D_MODEL), lambda i, j: (0, 0)),
            pl.BlockSpec((D_MODEL, PROJ_TILE), lambda i, j: (0, j)),
            pl.BlockSpec((D_MODEL, LANES), lambda i, j: (0, 0)),
            pl.BlockSpec((1, HEAD_DIM), lambda i, j: (0, 0)),
            pl.BlockSpec((1, HEAD_DIM), lambda i, j: (0, 0)),
            pl.BlockSpec((tm, HEAD_DIM), lambda i, j: (ridx(i), 0)),
            pl.BlockSpec((tm, HEAD_DIM), lambda i, j: (ridx(i), 0)),
        ],
        out_specs=[
            pl.BlockSpec((tm, PROJ_TILE), lambda i, j: (i, j)),
            pl.BlockSpec((tm, LANES), lambda i, j: (i, 0)),
            pl.BlockSpec((tm, 2 * PROJ_TN), lambda i, j: (i, 0), pipeline_mode=pl.Buffered(1)),
        ],
        out_shape=[
            jax.ShapeDtypeStruct((t_all, MAIN_COLS), bf16),
            jax.ShapeDtypeStruct((t_all, LANES), f32),
            jax.ShapeDtypeStruct((t_all, 2 * PROJ_TN), f32),
        ],
        scratch_shapes=[pltpu.VMEM((tm, D_MODEL), bf16)],
        compiler_params=_params(("arbitrary", "arbitrary")),
        name="in_proj",
    )(x_all, mod, norm1, w_main, w_lr, qg, kg, cos_t, sin_t)


def _attn_kernel(*refs, has_ctx):
    if has_ctx:
        q_ref, k_ref, v_ref, ck_ref, cv_ref, _, o_ref, va_ref, ckb_ref, cva_ref = refs
    else:
        q_ref, k_ref, v_ref, _, o_ref, va_ref = refs

    @pl.when(pl.program_id(2) == 0)
    def _():
        va_ref[:, :HEAD_DIM] = v_ref[...]
        va_ref[:, HEAD_DIM:] = jnp.ones((va_ref.shape[0], HEAD_DIM), bf16)
        if has_ctx:
            ckb_ref[...] = ck_ref[...].astype(bf16)
            cva_ref[:, :HEAD_DIM] = cv_ref[...].astype(bf16)
            cva_ref[:, HEAD_DIM:] = jnp.ones((cva_ref.shape[0], HEAD_DIM), bf16)

    k = k_ref[...]

    def scores(g):
        q = q_ref[:, g * HEAD_DIM:(g + 1) * HEAD_DIM]
        s = lax.dot_general(q, k, NT_DIMS, preferred_element_type=f32)
        sc = lax.dot_general(q, ckb_ref[...], NT_DIMS, preferred_element_type=f32) if has_ctx else None
        return s, sc

    nxt = scores(0)
    for g in range(Q_GROUPS):
        s, sc = nxt
        if g + 1 < Q_GROUPS:
            nxt = scores(g + 1)
        m = jnp.max(s, axis=-1, keepdims=True)
        if has_ctx:
            m = jnp.maximum(m, jnp.max(sc, axis=-1, keepdims=True))
        oa = jnp.dot(jnp.exp2(s - m).astype(bf16), va_ref[...], preferred_element_type=f32)
        if has_ctx:
            oa = oa + jnp.dot(jnp.exp2(sc - m).astype(bf16), cva_ref[...], preferred_element_type=f32)
        o_ref[:, g * HEAD_DIM:(g + 1) * HEAD_DIM] = (oa[:, :HEAD_DIM] / oa[:, HEAD_DIM:]).astype(bf16)


def _attention(main, o_prev, row0, n_batch, seq, ctx=None):
    tq = _tile(seq, 512)
    nq = seq // tq
    assert row0 % seq == 0
    qb0 = row0 // tq
    kb0 = row0 // seq
    in_specs = [
        pl.BlockSpec((tq, PROJ_TN), lambda b, g, i: (qb0 + b * nq + i, g)),
        pl.BlockSpec((seq, HEAD_DIM), lambda b, g, i: (kb0 + b, COL_K // HEAD_DIM + g)),
        pl.BlockSpec((seq, HEAD_DIM), lambda b, g, i: (kb0 + b, COL_V // HEAD_DIM + g)),
    ]
    args = [main, main, main]
    if ctx is not None:
        ck, cv, layer = ctx
        past = ck.shape[2]
        spec = pl.BlockSpec((None, None, past, HEAD_DIM), lambda b, g, i: (b, layer, 0, g))
        in_specs += [spec, spec]
        args += [ck, cv]
    in_specs.append(pl.BlockSpec(memory_space=pl.ANY))
    args.append(o_prev)
    scratch = [pltpu.VMEM((seq, 2 * HEAD_DIM), bf16)]
    if ctx is not None:
        scratch += [pltpu.VMEM((past, HEAD_DIM), bf16), pltpu.VMEM((past, 2 * HEAD_DIM), bf16)]
    return pl.pallas_call(
        functools.partial(_attn_kernel, has_ctx=ctx is not None),
        grid=(n_batch, KV_HEADS, nq),
        in_specs=in_specs,
        out_specs=pl.BlockSpec((tq, PROJ_TN), lambda b, g, i: (qb0 + b * nq + i, g)),
        out_shape=jax.ShapeDtypeStruct(o_prev.shape, o_prev.dtype),
        scratch_shapes=scratch,
        input_output_aliases={len(args) - 1: 0},
        compiler_params=_params(("arbitrary", "arbitrary", "arbitrary")),
        name="attention_ctx" if ctx is not None else "attention",
    )(*args)


def _log_sigmoid(z):
    return jnp.minimum(z, 0.0) - jnp.log1p(jnp.exp(-jnp.abs(z)))


def _gla_block(q_ref, k_ref, v_ref, g, st_ref, o_ref, tri, mask, ref_row, last_row, order):
    n = len(order)
    q_scale = GLA_DK ** -0.5
    rows = {c: slice(c * GLA_CHUNK, (c + 1) * GLA_CHUNK) for c in order}
    b = {}
    for c in order:
        gc = g[rows[c], :]
        g1 = gc.astype(bf16)
        r1 = gc - g1.astype(f32)
        g2 = r1.astype(bf16)
        g3 = (r1 - g2.astype(f32)).astype(bf16)
        b[c] = (jnp.dot(tri, g1, preferred_element_type=f32) + jnp.dot(tri, g2, preferred_element_type=f32)
                + jnp.dot(tri, g3, preferred_element_type=f32))
        yield
    qd, kd, qa, ka, dec = {}, {}, {}, {}, {}
    for c in order:
        q = q_ref[rows[c], :].astype(f32) * q_scale
        k = k_ref[rows[c], :].astype(f32)
        b_ref = b[c][ref_row:ref_row + 1, :]
        b_last = b[c][last_row:last_row + 1, :]
        qd[c] = q * jnp.exp(b[c])
        kd[c] = k * jnp.exp(b_last - b[c])
        qa[c] = (q * jnp.exp(b[c] - b_ref)).astype(bf16)
        ka[c] = (k * jnp.exp(b_ref - b[c])).astype(bf16)
        dec[c] = jnp.exp(b_last)
        yield
    zero = jnp.zeros((GLA_CHUNK, GLA_CHUNK), bf16)
    blocks = [[zero] * n for _ in range(n)]
    for c in order:
        a = lax.dot_general(qa[c], ka[c], NT_DIMS, preferred_element_type=f32)
        blocks[c][c] = jnp.where(mask, a, 0.0).astype(bf16)
        yield
    q_rows = [None] * n
    k_rows = [None] * n
    into = None
    for pi, ci in enumerate(order):
        q_rows[ci] = (qd[ci] if into is None else qd[ci] * into).astype(bf16)
        between = None
        for pj in range(pi - 1, -1, -1):
            cj = order[pj]
            qs = (qd[ci] if between is None else qd[ci] * between).astype(bf16)
            blocks[ci][cj] = lax.dot_general(qs, kd[cj].astype(bf16), NT_DIMS,
                                             preferred_element_type=f32).astype(bf16)
            between = dec[cj] if between is None else between * dec[cj]
        into = dec[ci] if into is None else into * dec[ci]
        yield
    after = None
    for ci in reversed(order):
        k_rows[ci] = (kd[ci] if after is None else kd[ci] * after).astype(bf16)
        after = dec[ci] if after is None else after * dec[ci]
    yield
    a_full = jnp.concatenate([jnp.concatenate(r, axis=1) for r in blocks], axis=0)
    q_blk = jnp.concatenate(q_rows, axis=0)
    v = v_ref[...]
    st = st_ref[...]
    o = (jnp.dot(a_full, v, preferred_element_type=f32)
         + lax.dot_general(q_blk, st.astype(bf16), NT_DIMS, preferred_element_type=f32))
    o_ref[...] = o.astype(o_ref.dtype)
    yield
    k_blk = jnp.concatenate(k_rows, axis=0)
    st_ref[...] = st * into + lax.dot_general(v, k_blk, TN_DIMS, preferred_element_type=f32)


def _gla_kernel(*refs, has_state, emit_state):
    (qf_ref, kf_ref, vf_ref, lf_ref, qb_ref, kb_ref, vb_ref, lb_ref,
     wf_ref, wb_ref, bf_ref, bb_ref) = refs[:12]
    refs = refs[12:]
    if has_state:
        s0f_ref, s0b_ref = refs[:2]
        refs = refs[2:]
    refs = refs[2:]
    of_ref, ob_ref = refs[:2]
    refs = refs[2:]
    if emit_state:
        sf_ref, sb_ref = refs[:2]
        refs = refs[2:]
    stf_ref, stb_ref = refs
    c = pl.program_id(2)
    heads = stf_ref.shape[0]

    @pl.when(c == 0)
    def _():
        for hh in range(heads):
            if has_state:
                stf_ref[hh] = s0f_ref[hh].T
                stb_ref[hh] = s0b_ref[hh].T
            else:
                stf_ref[hh] = jnp.zeros((GLA_DV, GLA_DK), f32)
                stb_ref[hh] = jnp.zeros((GLA_DV, GLA_DK), f32)

    row = lax.broadcasted_iota(jnp.int32, (GLA_CHUNK, GLA_CHUNK), 0)
    col = lax.broadcasted_iota(jnp.int32, (GLA_CHUNK, GLA_CHUNK), 1)
    lower = col <= row
    upper = col >= row
    tri_f = jnp.where(lower, 1.0, 0.0).astype(bf16)
    tri_b = jnp.where(upper, 1.0, 0.0).astype(bf16)
    mid = GLA_CHUNK // 2
    n_chunks = GLA_BLOCK // GLA_CHUNK

    def decay(l_ref, w_ref, b_ref):
        z = jnp.dot(l_ref[...].astype(bf16), w_ref[...], preferred_element_type=f32) + b_ref[...]
        return _log_sigmoid(z) * (1.0 / GK_NORMALIZER)

    g_f = decay(lf_ref, wf_ref, bf_ref)
    g_b = decay(lb_ref, wb_ref, bb_ref)
    order = list(range(n_chunks))
    chains = []
    for hh in range(heads):
        ks = slice(hh * GLA_DK, (hh + 1) * GLA_DK)
        vs = slice(hh * GLA_DV, (hh + 1) * GLA_DV)
        chains.append(_gla_block(qf_ref.at[:, ks], kf_ref.at[:, ks], vf_ref.at[:, vs], g_f[:, ks],
                                 stf_ref.at[hh], of_ref.at[:, vs], tri_f, lower, mid, GLA_CHUNK - 1, order))
        chains.append(_gla_block(qb_ref.at[:, ks], kb_ref.at[:, ks], vb_ref.at[:, vs], g_b[:, ks],
                                 stb_ref.at[hh], ob_ref.at[:, vs], tri_b, upper, GLA_CHUNK - 1 - mid, 0,
                                 order[::-1]))
    while chains:
        chains = [ch for ch in chains if next(ch, True) is None]

    if emit_state:
        @pl.when(c == pl.num_programs(2) - 1)
        def _():
            for hh in range(heads):
                sf_ref[hh] = stf_ref[hh].T
                sb_ref[hh] = stb_ref[hh].T


def _gla(main, lr, of_prev, ob_prev, wgk_f, wgk_b, bgk_f, bgk_b, row0, n_batch, seq,
         states=None, emit_state=False):
    nb = seq // GLA_BLOCK
    assert row0 % GLA_BLOCK == 0
    rb0 = row0 // GLA_BLOCK
    fwd = lambda b, c: rb0 + b * nb + c
    bwd = lambda b, c: rb0 + b * nb + (nb - 1 - c)

    hps = GLA_HEADS_PER_STEP
    dk, dv = hps * GLA_DK, hps * GLA_DV

    def token_specs(pos):
        return [
            pl.BlockSpec((GLA_BLOCK, dk), lambda b, h, c: (pos(b, c), COL_GQ // dk + h)),
            pl.BlockSpec((GLA_BLOCK, dk), lambda b, h, c: (pos(b, c), COL_GK // dk + h)),
            pl.BlockSpec((GLA_BLOCK, dv), lambda b, h, c: (pos(b, c), COL_GV // dv + h)),
            pl.BlockSpec((GLA_BLOCK, LANES), lambda b, h, c: (pos(b, c), 0)),
        ]

    in_specs = token_specs(fwd) + token_specs(bwd) + [
        pl.BlockSpec((LANES, dk), lambda b, h, c: (0, h)),
        pl.BlockSpec((LANES, dk), lambda b, h, c: (0, h)),
        pl.BlockSpec((1, dk), lambda b, h, c: (0, h)),
        pl.BlockSpec((1, dk), lambda b, h, c: (0, h)),
    ]
    args = [main, main, main, lr, main, main, main, lr, wgk_f, wgk_b, bgk_f, bgk_b]
    if states is not None:
        s_f, s_b, layer = states
        spec = pl.BlockSpec((None, None, hps, GLA_DK, GLA_DV), lambda b, h, c: (b, layer, h, 0, 0))
        in_specs += [spec, spec]
        args += [s_f, s_b]
    in_specs += [pl.BlockSpec(memory_space=pl.ANY)] * 2
    alias0 = len(args)
    args += [of_prev, ob_prev]
    out_specs = [
        pl.BlockSpec((GLA_BLOCK, dv), lambda b, h, c: (fwd(b, c), h)),
        pl.BlockSpec((GLA_BLOCK, dv), lambda b, h, c: (bwd(b, c), h)),
    ]
    out_shape = [jax.ShapeDtypeStruct(of_prev.shape, of_prev.dtype),
                 jax.ShapeDtypeStruct(ob_prev.shape, ob_prev.dtype)]
    if emit_state:
        spec = pl.BlockSpec((None, hps, GLA_DK, GLA_DV), lambda b, h, c: (b, h, 0, 0))
        out_specs += [spec, spec]
        out_shape += [jax.ShapeDtypeStruct((n_batch, GLA_HEADS, GLA_DK, GLA_DV), f32)] * 2
    return pl.pallas_call(
        functools.partial(_gla_kernel, has_state=states is not None, emit_state=emit_state),
        grid=(n_batch, GLA_HEADS // hps, nb),
        in_specs=in_specs,
        out_specs=out_specs,
        out_shape=out_shape,
        scratch_shapes=[pltpu.VMEM((hps, GLA_DV, GLA_DK), f32), pltpu.VMEM((hps, GLA_DV, GLA_DK), f32)],
        input_output_aliases={alias0: 0, alias0 + 1: 1},
        compiler_params=_params(("arbitrary", "arbitrary", "arbitrary")),
        name="gla_state" if states is not None else "gla",
    )(*args)


def _merge_kernel(x_ref, mod_ref, oa_ref, of_ref, ob_ref, gg_ref, ga_ref, gb_ref, gn_ref, w_ref, o_ref):
    mix = None
    for hh in range(GLA_HEADS):
        sl = slice(hh * GLA_DV, (hh + 1) * GLA_DV)
        og = of_ref[:, sl].astype(f32) + ob_ref[:, sl].astype(f32)
        gla = _rms(og, gn_ref[...]) * _silu(gg_ref[:, sl].astype(f32))
        merged = (jax.nn.sigmoid(ga_ref[:, sl].astype(f32)) * oa_ref[:, sl].astype(f32)
                  + jax.nn.sigmoid(gb_ref[:, sl].astype(f32)) * gla)
        part = jnp.dot(merged.astype(bf16), w_ref[sl, :], preferred_element_type=f32)
        mix = part if mix is None else mix + part
    o_ref[...] = x_ref[...] + mod_ref[2:3, :] * mix


def _merge_out(x_all, mod, layer, o_att, o_f, o_b, main, gla_norm, w_out, n_prompt, dec_seq):
    t_all = x_all.shape[0]
    tm = _tile(min(n_prompt, dec_seq), 256)
    cidx = _cond_index(tm, n_prompt, dec_seq)
    row = lambda i: (i, 0)
    return pl.pallas_call(
        _merge_kernel,
        grid=(t_all // tm,),
        in_specs=[
            pl.BlockSpec((tm, D_MODEL), row),
            pl.BlockSpec((None, None, 6, D_MODEL), lambda i: (layer, cidx(i), 0, 0)),
            pl.BlockSpec((tm, D_MODEL), row),
            pl.BlockSpec((tm, GLA_VAL), row),
            pl.BlockSpec((tm, GLA_VAL), row),
            pl.BlockSpec((tm, GLA_VAL), lambda i: (i, COL_GG // GLA_VAL)),
            pl.BlockSpec((tm, D_MODEL), lambda i: (i, COL_GA // D_MODEL)),
            pl.BlockSpec((tm, D_MODEL), lambda i: (i, COL_GB // D_MODEL)),
            pl.BlockSpec((1, GLA_DV), lambda i: (0, 0)),
            pl.BlockSpec((D_MODEL, D_MODEL), lambda i: (0, 0)),
        ],
        out_specs=pl.BlockSpec((tm, D_MODEL), row),
        out_shape=jax.ShapeDtypeStruct(x_all.shape, f32),
        compiler_params=_params(("arbitrary",)),
        name="merge_out",
    )(x_all, mod, o_att, o_f, o_b, main, main, main, gla_norm, w_out)


def _ffn_kernel(x_ref, mod_ref, n2_ref, wg_ref, wu_ref, wd_ref, o_ref, h_ref, acc_ref):
    j = pl.program_id(1)

    @pl.when(j == 0)
    def _():
        y = _rms(x_ref[...], n2_ref[...])
        h_ref[...] = (y * (1.0 + mod_ref[4:5, :]) + mod_ref[3:4, :]).astype(bf16)
        acc_ref[...] = jnp.zeros_like(acc_ref)

    h = h_ref[...]
    gate = jnp.dot(h, wg_ref[...], preferred_element_type=f32)
    up = jnp.dot(h, wu_ref[...], preferred_element_type=f32)
    acc_ref[...] += jnp.dot((_silu(gate) * up).astype(bf16), wd_ref[...], preferred_element_type=f32)

    @pl.when(j == pl.num_programs(1) - 1)
    def _():
        o_ref[...] = x_ref[...] + mod_ref[5:6, :] * acc_ref[...]


def _dense_ffn(x_all, mod, layer, norm2, wg, wu, wd, n_prompt, dec_seq):
    t_all = x_all.shape[0]
    tm = _tile(min(n_prompt, dec_seq), 512)
    tf = 512
    cidx = _cond_index(tm, n_prompt, dec_seq)
    return pl.pallas_call(
        _ffn_kernel,
        grid=(t_all // tm, FF_DENSE // tf),
        in_specs=[
            pl.BlockSpec((tm, D_MODEL), lambda i, j: (i, 0)),
            pl.BlockSpec((None, None, 6, D_MODEL), lambda i, j: (layer, cidx(i), 0, 0)),
            pl.BlockSpec((1, D_MODEL), lambda i, j: (0, 0)),
            pl.BlockSpec((D_MODEL, tf), lambda i, j: (0, j)),
            pl.BlockSpec((D_MODEL, tf), lambda i, j: (0, j)),
            pl.BlockSpec((tf, D_MODEL), lambda i, j: (j, 0)),
        ],
        out_specs=pl.BlockSpec((tm, D_MODEL), lambda i, j: (i, 0)),
        out_shape=jax.ShapeDtypeStruct(x_all.shape, f32),
        scratch_shapes=[pltpu.VMEM((tm, D_MODEL), bf16), pltpu.VMEM((tm, D_MODEL), f32)],
        compiler_params=_params(("arbitrary", "arbitrary")),
        name="dense_ffn",
    )(x_all, mod, norm2, wg, wu, wd)


INFO_E1, INFO_E2, INFO_W1, INFO_W2, INFO_R1, INFO_R2 = range(6)


def _route_kernel(x_ref, mod_ref, n2_ref, wr_ref, h_ref, info_ref, cnt_ref, run_ref):
    i = pl.program_id(0)
    tm = x_ref.shape[0]

    @pl.when(i == 0)
    def _():
        run_ref[...] = jnp.zeros_like(run_ref)

    y = _rms(x_ref[...], n2_ref[...])
    h = y * (1.0 + mod_ref[4:5, :]) + mod_ref[3:4, :]
    h_ref[...] = h
    logits = jnp.dot(h, wr_ref[...], preferred_element_type=f32, precision=lax.Precision.HIGHEST)
    lane = lax.broadcasted_iota(jnp.int32, (tm, LANES), 1)
    lane_f = lane.astype(f32)
    neg = -jnp.inf
    lg = jnp.where(lane < N_EXPERTS, logits, neg)
    m1 = jnp.max(lg, axis=-1, keepdims=True)
    i1 = jnp.min(jnp.where(lg == m1, lane_f, float(LANES)), axis=-1, keepdims=True)
    e1 = lane_f == i1
    lg2 = jnp.where(e1, neg, lg)
    m2 = jnp.max(lg2, axis=-1, keepdims=True)
    i2 = jnp.min(jnp.where(lg2 == m2, lane_f, float(LANES)), axis=-1, keepdims=True)
    e2 = lane_f == i2
    ex = jnp.exp(m2 - m1)
    w1 = 1.0 / (1.0 + ex)
    w2 = ex / (1.0 + ex)
    e1f = jnp.where(e1, 1.0, 0.0)
    e2f = jnp.where(e2, 1.0, 0.0)
    both = e1f + e2f
    row = lax.broadcasted_iota(jnp.int32, (tm, tm), 0)
    col = lax.broadcasted_iota(jnp.int32, (tm, tm), 1)
    strict = jnp.where(col < row, 1.0, 0.0).astype(bf16)
    before = jnp.dot(strict, both.astype(bf16), preferred_element_type=f32) + run_ref[...]
    r1 = jnp.sum(before * e1f, axis=-1, keepdims=True)
    r2 = jnp.sum(before * e2f, axis=-1, keepdims=True)
    run = run_ref[...] + jnp.sum(both, axis=0, keepdims=True)
    run_ref[...] = run
    cnt_ref[...] = run
    info = jnp.zeros((tm, LANES), f32)
    for idx, val in ((INFO_E1, i1), (INFO_E2, i2), (INFO_W1, w1), (INFO_W2, w2),
                     (INFO_R1, r1), (INFO_R2, r2)):
        info = jnp.where(lane == idx, val, info)
    info_ref[...] = info


def _route(x_all, mod, layer, norm2, w_router, n_prompt, dec_seq):
    t_all = x_all.shape[0]
    tm = _tile(min(n_prompt, dec_seq), 512)
    cidx = _cond_index(tm, n_prompt, dec_seq)
    return pl.pallas_call(
        _route_kernel,
        grid=(t_all // tm,),
        in_specs=[
            pl.BlockSpec((tm, D_MODEL), lambda i: (i, 0)),
            pl.BlockSpec((None, None, 6, D_MODEL), lambda i: (layer, cidx(i), 0, 0)),
            pl.BlockSpec((1, D_MODEL), lambda i: (0, 0)),
            pl.BlockSpec((D_MODEL, LANES), lambda i: (0, 0)),
        ],
        out_specs=[
            pl.BlockSpec((tm, D_MODEL), lambda i: (i, 0)),
            pl.BlockSpec((tm, LANES), lambda i: (i, 0)),
            pl.BlockSpec((1, LANES), lambda i: (0, 0)),
        ],
        out_shape=[
            jax.ShapeDtypeStruct((t_all, D_MODEL), f32),
            jax.ShapeDtypeStruct((t_all, LANES), f32),
            jax.ShapeDtypeStruct((1, LANES), f32),
        ],
        scratch_shapes=[pltpu.VMEM((1, LANES), f32)],
        compiler_params=_params(("arbitrary",)),
        name="route",
    )(x_all, mod, norm2, w_router)


def _row_copy(src, src_row, dst, dst_row, sem):
    return pltpu.make_async_copy(src.at[pl.ds(src_row, 1)], dst.at[pl.ds(dst_row, 1)], sem)


def _dispatch_kernel(tok_ref, h_hbm, o_ref, buf_ref, sem, *, rows):
    base = pl.program_id(0) * rows

    def issue(r, carry):
        _row_copy(h_hbm, tok_ref[base + r], buf_ref, r, sem).start()
        return carry

    lax.fori_loop(0, rows, issue, 0, unroll=DMA_UNROLL)

    def drain(r, carry):
        _row_copy(h_hbm, 0, buf_ref, 0, sem).wait()
        return carry

    lax.fori_loop(0, rows, drain, 0, unroll=DMA_UNROLL)
    o_ref[...] = buf_ref[...].astype(bf16)


def _dispatch(slot_token, h_all):
    n_rows = slot_token.shape[0]
    rows = _tile(n_rows, 1024)
    return pl.pallas_call(
        functools.partial(_dispatch_kernel, rows=rows),
        grid_spec=pltpu.PrefetchScalarGridSpec(
            num_scalar_prefetch=1,
            grid=(n_rows // rows,),
            in_specs=[pl.BlockSpec(memory_space=pl.ANY)],
            out_specs=pl.BlockSpec((rows, D_MODEL), lambda i, t: (i, 0)),
            scratch_shapes=[pltpu.VMEM((rows, D_MODEL), f32), pltpu.SemaphoreType.DMA(())],
        ),
        out_shape=jax.ShapeDtypeStruct((n_rows, D_MODEL), bf16),
        compiler_params=_params(("arbitrary",)),
        name="moe_dispatch",
    )(slot_token, h_all)


def _expert_kernel(be_ref, na_ref, x_ref, wg_ref, wu_ref, wd_ref, o_ref, acc_ref):
    b = pl.program_id(0)
    j = pl.program_id(1)
    active = b < na_ref[0]

    @pl.when(active & (j == 0))
    def _():
        acc_ref[...] = jnp.zeros_like(acc_ref)

    @pl.when(active)
    def _():
        h = x_ref[...]
        gate = jnp.dot(h, wg_ref[...], preferred_element_type=f32)
        up = jnp.dot(h, wu_ref[...], preferred_element_type=f32)
        acc_ref[...] += jnp.dot((_silu(gate) * up).astype(bf16), wd_ref[...], preferred_element_type=f32)

    @pl.when(j == pl.num_programs(1) - 1)
    def _():
        o_ref[...] = jnp.where(active, acc_ref[...], 0.0)


def _expert_ffn(block_expert, n_active, sorted_h, wg, wu, wd, tmb):
    n_rows = sorted_h.shape[0]
    tf = 1024
    nf = FF_EXPERT // tf

    def ff(b, j, na):
        return jnp.where(b < na[0], j, nf - 1)

    return pl.pallas_call(
        _expert_kernel,
        grid_spec=pltpu.PrefetchScalarGridSpec(
            num_scalar_prefetch=2,
            grid=(n_rows // tmb, nf),
            in_specs=[
                pl.BlockSpec((tmb, D_MODEL), lambda b, j, be, na: (b, 0)),
                pl.BlockSpec((None, D_MODEL, tf), lambda b, j, be, na: (be[b], 0, ff(b, j, na))),
                pl.BlockSpec((None, D_MODEL, tf), lambda b, j, be, na: (be[b], 0, ff(b, j, na))),
                pl.BlockSpec((None, tf, D_MODEL), lambda b, j, be, na: (be[b], ff(b, j, na), 0)),
            ],
            out_specs=pl.BlockSpec((tmb, D_MODEL), lambda b, j, be, na: (b, 0)),
            scratch_shapes=[pltpu.VMEM((tmb, D_MODEL), f32)],
        ),
        out_shape=jax.ShapeDtypeStruct((n_rows, D_MODEL), f32),
        compiler_params=_params(("arbitrary", "arbitrary")),
        name="expert_ffn",
    )(block_expert, n_active, sorted_h, wg, wu, wd)


def _combine_kernel(dest_ref, x_ref, mod_ref, info_ref, y_hbm, o_ref, buf_ref, sem, *, rows):
    base = pl.program_id(0) * rows

    def issue(r, carry):
        for kk in range(2):
            _row_copy(y_hbm, dest_ref[2 * (base + r) + kk], buf_ref.at[kk], r, sem).start()
        return carry

    lax.fori_loop(0, rows, issue, 0, unroll=DMA_UNROLL)

    def drain(r, carry):
        for kk in range(2):
            _row_copy(y_hbm, 0, buf_ref.at[kk], 0, sem).wait()
        return carry

    lax.fori_loop(0, rows, drain, 0, unroll=DMA_UNROLL)
    info = info_ref[...]
    w1 = info[:, INFO_W1:INFO_W1 + 1]
    w2 = info[:, INFO_W2:INFO_W2 + 1]
    o_ref[...] = x_ref[...] + mod_ref[5:6, :] * (w1 * buf_ref[0] + w2 * buf_ref[1])


def _combine(dest, x_all, mod, layer, info, y_sorted, n_prompt, dec_seq):
    t_all = x_all.shape[0]
    rows = _tile(min(n_prompt, dec_seq), 512)
    cidx = _cond_index(rows, n_prompt, dec_seq)
    return pl.pallas_call(
        functools.partial(_combine_kernel, rows=rows),
        grid_spec=pltpu.PrefetchScalarGridSpec(
            num_scalar_prefetch=1,
            grid=(t_all // rows,),
            in_specs=[
                pl.BlockSpec((rows, D_MODEL), lambda i, d: (i, 0)),
                pl.BlockSpec((None, None, 6, D_MODEL), lambda i, d: (layer, cidx(i), 0, 0)),
                pl.BlockSpec((rows, LANES), lambda i, d: (i, 0)),
                pl.BlockSpec(memory_space=pl.ANY),
            ],
            out_specs=pl.BlockSpec((rows, D_MODEL), lambda i, d: (i, 0)),
            scratch_shapes=[pltpu.VMEM((2, rows, D_MODEL), f32), pltpu.SemaphoreType.DMA(())],
        ),
        out_shape=jax.ShapeDtypeStruct(x_all.shape, f32),
        compiler_params=_params(("arbitrary",)),
        name="moe_combine",
    )(dest, x_all, mod, info, y_sorted)


def _moe_ffn(x_all, mod, layer, norm2, w_router, wg, wu, wd, n_prompt, dec_seq):
    t_all = x_all.shape[0]
    tmb = _tile(t_all, 512)
    n_blocks = (2 * t_all) // tmb + N_EXPERTS
    h_all, info, counts = _route(x_all, mod, layer, norm2, w_router, n_prompt, dec_seq)
    counts = counts[0, :N_EXPERTS].astype(jnp.int32)
    padded = ((counts + tmb - 1) // tmb) * tmb
    end = jnp.cumsum(padded)
    start = end - padded
    e12 = info[:, INFO_E1:INFO_E2 + 1].astype(jnp.int32)
    r12 = info[:, INFO_R1:INFO_R2 + 1].astype(jnp.int32)
    dest = (start[e12] + r12).reshape(2 * t_all)
    slot_token = jnp.zeros((n_blocks * tmb,), jnp.int32).at[dest].set(
        jnp.arange(2 * t_all, dtype=jnp.int32) // 2, unique_indices=True)
    n_active = (end[-1] // tmb).astype(jnp.int32).reshape(1)
    blk = jnp.minimum(jnp.arange(n_blocks, dtype=jnp.int32), n_active[0] - 1) * tmb
    block_expert = jnp.minimum(jnp.sum(blk[:, None] >= end[None, :], axis=1), N_EXPERTS - 1).astype(jnp.int32)
    sorted_h = _dispatch(slot_token, h_all)
    y_sorted = _expert_ffn(block_expert, n_active, sorted_h, wg, wu, wd, tmb)
    return _combine(dest, x_all, mod, layer, info, y_sorted, n_prompt, dec_seq)


def _final_kernel(x_ref, g_ref, o_ref):
    o_ref[...] = _rms(x_ref[...], g_ref[...])


def _final_norm(x_all, gain, row0, n_rows):
    tm = _tile(n_rows, 512)
    assert row0 % tm == 0
    return pl.pallas_call(
        _final_kernel,
        grid=(n_rows // tm,),
        in_specs=[pl.BlockSpec((tm, D_MODEL), lambda i: (row0 // tm + i, 0)),
                  pl.BlockSpec((1, D_MODEL), lambda i: (0, 0))],
        out_specs=pl.BlockSpec((tm, D_MODEL), lambda i: (i, 0)),
        out_shape=jax.ShapeDtypeStruct((n_rows, D_MODEL), f32),
        compiler_params=_params(("arbitrary",)),
        name="final_norm",
    )(x_all, gain)


def _rope_tables(dec_seq, tile_rows):
    rows = dec_seq // GRID_W
    row = jnp.repeat(jnp.arange(rows, dtype=f32), GRID_W)
    col = jnp.tile(jnp.arange(GRID_W, dtype=f32), rows)
    freqs = ROPE_THETA ** (-jnp.arange(AXIS_PAIRS, dtype=f32) / AXIS_PAIRS)
    ang = jnp.concatenate([row[:, None] * freqs, col[:, None] * freqs], axis=-1)
    cos, sin = jnp.cos(ang), jnp.sin(ang)
    cos_t = jnp.concatenate([cos, cos], axis=-1)
    sin_t = jnp.concatenate([-sin, sin], axis=-1)
    cos_t = jnp.concatenate([cos_t, jnp.ones((tile_rows, HEAD_DIM), f32)], axis=0)
    sin_t = jnp.concatenate([sin_t, jnp.zeros((tile_rows, HEAD_DIM), f32)], axis=0)
    return cos_t, sin_t


def _split_w_in(w):
    sizes = (ATT_HEADS * HEAD_DIM, KV_HEADS * HEAD_DIM, KV_HEADS * HEAD_DIM, GLA_KEY, GLA_KEY, GLA_VAL,
             GLA_VAL, GK_RANK, GK_RANK, D_MODEL, D_MODEL)
    offs = np.concatenate([[0], np.cumsum(sizes)])
    q, k, v, gq, gk, gv, gg, lrf, lrb, ga, gb = (w[:, offs[n]:offs[n + 1]] for n in range(len(sizes)))
    main = jnp.concatenate([q, gv, gg, ga, gb, gq, gk, k, v], axis=1).astype(bf16)
    lr = jnp.concatenate([lrf, lrb, jnp.zeros((D_MODEL, LANES - 2 * GK_RANK), w.dtype)], axis=1).astype(bf16)
    return main, lr


def kernel(x_prompt, x_sample, cache_k, cache_v, state_fwd, state_bwd, c, c_ctx, w_mod, b_mod, norm1, w_in,
           q_norm, k_norm, w_gk, b_gk, gla_norm, w_out, norm2, w_ff_gate, w_ff_up, w_ff_down, w_router,
           w_e_gate, w_e_up, w_e_down, final_norm):
    batch, seq, _ = x_prompt.shape
    dec_batch, dec_seq, _ = x_sample.shape
    depth = w_mod.shape[0]
    past = cache_k.shape[2]
    n_prompt = batch * seq
    n_sample = dec_batch * dec_seq
    t_all = n_prompt + n_sample
    assert dec_batch + 1 <= N_COND_PAD and n_prompt % dec_seq == 0 and seq % GLA_BLOCK == 0

    x_all = jnp.concatenate([x_prompt.reshape(n_prompt, D_MODEL), x_sample.reshape(n_sample, D_MODEL)], axis=0)
    cond = jnp.concatenate([c_ctx[None, :], c, jnp.zeros((N_COND_PAD - 1 - dec_batch, D_MODEL), f32)], axis=0)
    mod = _modulation(cond, w_mod, b_mod)

    proj_tm = _tile(min(n_prompt, dec_seq), 1024)
    cos_t, sin_t = _rope_tables(dec_seq, proj_tm)
    ck = cache_k.reshape(dec_batch, depth, past, KV_HEADS * HEAD_DIM)
    cv = cache_v.reshape(dec_batch, depth, past, KV_HEADS * HEAD_DIM)

    o_att = jnp.zeros((t_all, D_MODEL), bf16)
    o_f = jnp.zeros((t_all, GLA_VAL), bf16)
    o_b = jnp.zeros((t_all, GLA_VAL), bf16)
    new_k, new_v, new_sf, new_sb = [], [], [], []
    for l in range(depth):
        w_main, w_lr = _split_w_in(w_in[l])
        main, lr, kv32 = _in_proj(x_all, mod, l, norm1[l][None, :], w_main, w_lr, q_norm[l][None, :],
                                  k_norm[l][None, :], cos_t, sin_t, n_prompt, dec_seq)
        new_k.append(kv32[:n_prompt, :PROJ_TN].reshape(batch, seq, KV_HEADS, HEAD_DIM))
        new_v.append(kv32[:n_prompt, PROJ_TN:].reshape(batch, seq, KV_HEADS, HEAD_DIM))

        o_att = _attention(main, o_att, 0, batch, seq)
        o_att = _attention(main, o_att, n_prompt, dec_batch, dec_seq, ctx=(ck, cv, l))

        pad = jnp.zeros((LANES - GK_RANK, GLA_KEY), f32)
        wgk_f = jnp.concatenate([w_gk[l, 0], pad], axis=0).astype(bf16)
        wgk_b = jnp.concatenate([pad[:GK_RANK], w_gk[l, 1], pad[:LANES - 2 * GK_RANK]], axis=0).astype(bf16)
        bgk_f, bgk_b = b_gk[l, 0][None, :], b_gk[l, 1][None, :]
        o_f, o_b, s_f, s_b = _gla(main, lr, o_f, o_b, wgk_f, wgk_b, bgk_f, bgk_b, 0, batch, seq,
                                  emit_state=True)
        new_sf.append(s_f)
        new_sb.append(s_b)
        o_f, o_b = _gla(main, lr, o_f, o_b, wgk_f, wgk_b, bgk_f, bgk_b, n_prompt, dec_batch, dec_seq,
                        states=(state_fwd, state_bwd, l))

        x_all = _merge_out(x_all, mod, l, o_att, o_f, o_b, main, gla_norm[l][None, :], w_out[l].astype(bf16),
                           n_prompt, dec_seq)
        i = l // 2
        if l % 2 == 1:
            w_r = jnp.concatenate([w_router[i], jnp.zeros((D_MODEL, LANES - N_EXPERTS), f32)], axis=1)
            x_all = _moe_ffn(x_all, mod, l, norm2[l][None, :], w_r, w_e_gate[i].astype(bf16),
                             w_e_up[i].astype(bf16), w_e_down[i].astype(bf16), n_prompt, dec_seq)
        else:
            x_all = _dense_ffn(x_all, mod, l, norm2[l][None, :], w_ff_gate[i].astype(bf16),
                               w_ff_up[i].astype(bf16), w_ff_down[i].astype(bf16), n_prompt, dec_seq)

    y_prompt = _final_norm(x_all, final_norm[None, :], 0, n_prompt).reshape(batch, seq, D_MODEL)
    y_sample = _final_norm(x_all, final_norm[None, :], n_prompt, n_sample).reshape(dec_batch, dec_seq, D_MODEL)
    return (y_prompt, y_sample, jnp.stack(new_k, axis=1), jnp.stack(new_v, axis=1),
            jnp.stack(new_sf, axis=1), jnp.stack(new_sb, axis=1))
```

```python
import functools

import jax
import jax.numpy as jnp
import numpy as np
from jax import lax
from jax.experimental import pallas as pl
from jax.experimental.pallas import tpu as pltpu

D_MODEL = 2048
GRID_W = 64
ATT_HEADS = 16
KV_HEADS = 4
HEAD_DIM = 128
Q_GROUPS = ATT_HEADS // KV_HEADS
ROPE_THETA = 10000.0
AXIS_PAIRS = HEAD_DIM // 4
GLA_HEADS = 4
GLA_DK = D_MODEL // 8
GLA_DV = D_MODEL // 4
GLA_KEY = GLA_HEADS * GLA_DK
GLA_VAL = GLA_HEADS * GLA_DV
GK_RANK = 16
GK_NORMALIZER = 16.0
GLA_CHUNK = 64
FF_DENSE = ((8 * D_MODEL // 3 + 255) // 256) * 256
N_EXPERTS = 8
FF_EXPERT = 7 * D_MODEL // 2
EPS = 1e-6
LOG2_E = 1.4426950408889634

LANES = 128
N_COND_PAD = 16
VMEM_LIMIT = 56 * 1024 * 1024

COL_Q = 0
COL_GV = COL_Q + ATT_HEADS * HEAD_DIM
COL_GG = COL_GV + GLA_VAL
COL_GA = COL_GG + GLA_VAL
COL_GB = COL_GA + D_MODEL
COL_GQ = COL_GB + D_MODEL
COL_GK = COL_GQ + GLA_KEY
COL_K = COL_GK + GLA_KEY
COL_V = COL_K + KV_HEADS * HEAD_DIM
MAIN_COLS = COL_V + KV_HEADS * HEAD_DIM
PROJ_TN = Q_GROUPS * HEAD_DIM
PROJ_TILE = 2 * PROJ_TN
assert COL_V + PROJ_TN == MAIN_COLS and COL_K % PROJ_TILE == 0 and COL_GV % PROJ_TILE == 0
GLA_BLOCK = 4 * GLA_CHUNK
GLA_HEADS_PER_STEP = 4
PROJ_ROW_CHUNK = 256
DMA_UNROLL = 8

f32 = jnp.float32
bf16 = jnp.bfloat16
NT_DIMS = (((1,), (1,)), ((), ()))
TN_DIMS = (((0,), (0,)), ((), ()))


def _params(semantics):
    return pltpu.CompilerParams(dimension_semantics=semantics, vmem_limit_bytes=VMEM_LIMIT)


def _tile(n, pref):
    t = pref
    while n % t:
        t //= 2
    return t


def _cond_index(tile_rows, n_prompt, dec_seq):
    npt = n_prompt // tile_rows
    per = dec_seq // tile_rows
    return lambda i: jnp.where(i < npt, 0, 1 + (i - npt) // per)


def _silu(x):
    return x * jax.nn.sigmoid(x)


def _rms(x, gain):
    return x * lax.rsqrt(jnp.mean(x * x, axis=-1, keepdims=True) + EPS) * gain


def _mod_kernel(c_ref, w_ref, b_ref, o_ref):
    s = _silu(c_ref[...]).astype(bf16)
    o_ref[...] = jnp.dot(s, w_ref[...].astype(bf16), preferred_element_type=f32) + b_ref[...]


def _modulation(cond, w_mod, b_mod):
    depth = w_mod.shape[0]
    tn = 1024
    out = pl.pallas_call(
        _mod_kernel,
        grid=(depth, 6 * D_MODEL // tn),
        in_specs=[
            pl.BlockSpec((N_COND_PAD, D_MODEL), lambda l, n: (0, 0)),
            pl.BlockSpec((None, D_MODEL, tn), lambda l, n: (l, 0, n)),
            pl.BlockSpec((None, 1, tn), lambda l, n: (l, 0, n)),
        ],
        out_specs=pl.BlockSpec((None, N_COND_PAD, tn), lambda l, n: (l, 0, n)),
        out_shape=jax.ShapeDtypeStruct((depth, N_COND_PAD, 6 * D_MODEL), f32),
        compiler_params=_params(("arbitrary", "arbitrary")),
        name="modulation",
    )(cond, w_mod, b_mod.reshape(depth, 1, 6 * D_MODEL))
    return out.reshape(depth, N_COND_PAD, 6, D_MODEL)


def _in_proj_kernel(x_ref, mod_ref, n1_ref, w_ref, wlr_ref, qg_ref, kg_ref, cos_ref, sin_ref,
                    main_ref, lr_ref, kv_ref, h_ref):
    j = pl.program_id(1)
    jq = COL_GV // PROJ_TILE
    jkv = COL_K // PROJ_TILE
    kw = KV_HEADS * HEAD_DIM

    @pl.when(j == 0)
    def _():
        y = _rms(x_ref[...], n1_ref[...])
        hb = (y * (1.0 + mod_ref[1:2, :]) + mod_ref[0:1, :]).astype(bf16)
        h_ref[...] = hb
        lr_ref[...] = jnp.dot(hb, wlr_ref[...], preferred_element_type=f32)

    tm = h_ref.shape[0]
    n_row_chunks = max(1, tm // PROJ_ROW_CHUNK)
    rc = tm // n_row_chunks

    def run(epilogue):
        chunk = lambda r: jnp.dot(h_ref[r * rc:(r + 1) * rc, :], w_ref[...], preferred_element_type=f32)
        nxt = chunk(0)
        for r in range(n_row_chunks):
            acc = nxt
            if r + 1 < n_row_chunks:
                nxt = chunk(r + 1)
            epilogue(slice(r * rc, (r + 1) * rc), acc)

    def normed_and_rotated(rows, acc, sl, gain):
        n = _rms(acc[:, sl], gain)
        return n, n * cos_ref[rows, :] + pltpu.roll(n, HEAD_DIM // 2, 1) * sin_ref[rows, :]

    def q_epilogue(rows, acc):
        for hh in range(PROJ_TILE // HEAD_DIM):
            sl = slice(hh * HEAD_DIM, (hh + 1) * HEAD_DIM)
            _, r = normed_and_rotated(rows, acc, sl, qg_ref[...])
            main_ref[rows, sl] = (r * (HEAD_DIM ** -0.5 * LOG2_E)).astype(bf16)

    def kv_epilogue(rows, acc):
        for hh in range(KV_HEADS):
            sl = slice(hh * HEAD_DIM, (hh + 1) * HEAD_DIM)
            n, r = normed_and_rotated(rows, acc, sl, kg_ref[...])
            kv_ref[rows, sl] = n
            main_ref[rows, sl] = r.astype(bf16)
        kv_ref[rows, kw:] = acc[:, kw:]
        main_ref[rows, kw:] = acc[:, kw:].astype(bf16)

    def plain_epilogue(rows, acc):
        main_ref[rows, :] = acc.astype(bf16)

    pl.when(j < jq)(lambda: run(q_epilogue))
    pl.when(j == jkv)(lambda: run(kv_epilogue))
    pl.when((j >= jq) & (j != jkv))(lambda: run(plain_epilogue))


def _in_proj(x_all, mod, layer, norm1, w_main, w_lr, qg, kg, cos_t, sin_t, n_prompt, dec_seq):
    t_all = x_all.shape[0]
    tm = _tile(min(n_prompt, dec_seq), 1024)
    cidx = _cond_index(tm, n_prompt, dec_seq)
    npt = n_prompt // tm
    per = dec_seq // tm
    ridx = lambda i: jnp.where(i < npt, per, (i - npt) % per)
    return pl.pallas_call(
        _in_proj_kernel,
        grid=(t_all // tm, MAIN_COLS // PROJ_TILE),
        in_specs=[
            pl.BlockSpec((tm, D_MODEL), lambda i, j: (i, 0)),
            pl.BlockSpec((None, None, 6, D_MODEL), lambda i, j: (layer, cidx(i), 0, 0)),
            pl.BlockSpec((1, D_MODEL), lambda i, j: (0, 0)),
            pl.BlockSpec((D_MODEL, PROJ_TILE), lambda i, j: (0, j)),
            pl.BlockSpec((D_MODEL, LANES), lambda i, j: (0, 0)),
            pl.BlockSpec((1, HEAD_DIM), lambda i, j: (0, 0)),
            pl.BlockSpec((1, HEAD_DIM), lambda i, j: (0, 0)),
            pl.BlockSpec((tm, HEAD_DIM), lambda i, j: (ridx(i), 0)),
            pl.BlockSpec((tm, HEAD_DIM), lambda i, j: (ridx(i), 0)),
        ],
        out_specs=[
            pl.BlockSpec((tm, PROJ_TILE), lambda i, j: (i, j)),
            pl.BlockSpec((tm, LANES), lambda i, j: (i, 0)),
            pl.BlockSpec((tm, 2 * PROJ_TN), lambda i, j: (i, 0), pipeline_mode=pl.Buffered(1)),
        ],
        out_shape=[
            jax.ShapeDtypeStruct((t_all, MAIN_COLS), bf16),
            jax.ShapeDtypeStruct((t_all, LANES), f32),
            jax.ShapeDtypeStruct((t_all, 2 * PROJ_TN), f32),
        ],
        scratch_shapes=[pltpu.VMEM((tm, D_MODEL), bf16)],
        compiler_params=_params(("arbitrary", "arbitrary")),
        name="in_proj",
    )(x_all, mod, norm1, w_main, w_lr, qg, kg, cos_t, sin_t)


def _attn_kernel(*refs, has_ctx):
    if has_ctx:
        q_ref, k_ref, v_ref, ck_ref, cv_ref, _, o_ref, va_ref, ckb_ref, cva_ref = refs
    else:
        q_ref, k_ref, v_ref, _, o_ref, va_ref = refs

    @pl.when(pl.program_id(2) == 0)
    def _():
        va_ref[:, :HEAD_DIM] = v_ref[...]
        va_ref[:, HEAD_DIM:] = jnp.ones((va_ref.shape[0], HEAD_DIM), bf16)
        if has_ctx:
            ckb_ref[...] = ck_ref[...].astype(bf16)
            cva_ref[:, :HEAD_DIM] = cv_ref[...].astype(bf16)
            cva_ref[:, HEAD_DIM:] = jnp.ones((cva_ref.shape[0], HEAD_DIM), bf16)

    k = k_ref[...]

    def scores(g):
        q = q_ref[:, g * HEAD_DIM:(g + 1) * HEAD_DIM]
        s = lax.dot_general(q, k, NT_DIMS, preferred_element_type=f32)
        sc = lax.dot_general(q, ckb_ref[...], NT_DIMS, preferred_element_type=f32) if has_ctx else None
        return s, sc

    nxt = scores(0)
    for g in range(Q_GROUPS):
        s, sc = nxt
        if g + 1 < Q_GROUPS:
            nxt = scores(g + 1)
        m = jnp.max(s, axis=-1, keepdims=True)
        if has_ctx:
            m = jnp.maximum(m, jnp.max(sc, axis=-1, keepdims=True))
        oa = jnp.dot(jnp.exp2(s - m).astype(bf16), va_ref[...], preferred_element_type=f32)
        if has_ctx:
            oa = oa + jnp.dot(jnp.exp2(sc - m).astype(bf16), cva_ref[...], preferred_element_type=f32)
        o_ref[:, g * HEAD_DIM:(g + 1) * HEAD_DIM] = (oa[:, :HEAD_DIM] / oa[:, HEAD_DIM:]).astype(bf16)


def _attention(main, o_prev, row0, n_batch, seq, ctx=None):
    tq = _tile(seq, 512)
    nq = seq // tq
    assert row0 % seq == 0
    qb0 = row0 // tq
    kb0 = row0 // seq
    in_specs = [
        pl.BlockSpec((tq, PROJ_TN), lambda b, g, i: (qb0 + b * nq + i, g)),
        pl.BlockSpec((seq, HEAD_DIM), lambda b, g, i: (kb0 + b, COL_K // HEAD_DIM + g)),
        pl.BlockSpec((seq, HEAD_DIM), lambda b, g, i: (kb0 + b, COL_V // HEAD_DIM + g)),
    ]
    args = [main, main, main]
    if ctx is not None:
        ck, cv, layer = ctx
        past = ck.shape[2]
        spec = pl.BlockSpec((None, None, past, HEAD_DIM), lambda b, g, i: (b, layer, 0, g))
        in_specs += [spec, spec]
        args += [ck, cv]
    in_specs.append(pl.BlockSpec(memory_space=pl.ANY))
    args.append(o_prev)
    scratch = [pltpu.VMEM((seq, 2 * HEAD_DIM), bf16)]
    if ctx is not None:
        scratch += [pltpu.VMEM((past, HEAD_DIM), bf16), pltpu.VMEM((past, 2 * HEAD_DIM), bf16)]
    return pl.pallas_call(
        functools.partial(_attn_kernel, has_ctx=ctx is not None),
        grid=(n_batch, KV_HEADS, nq),
        in_specs=in_specs,
        out_specs=pl.BlockSpec((tq, PROJ_TN), lambda b, g, i: (qb0 + b * nq + i, g)),
        out_shape=jax.ShapeDtypeStruct(o_prev.shape, o_prev.dtype),
        scratch_shapes=scratch,
        input_output_aliases={len(args) - 1: 0},
        compiler_params=_params(("arbitrary", "arbitrary", "arbitrary")),
        name="attention_ctx" if ctx is not None else "attention",
    )(*args)


def _log_sigmoid(z):
    return jnp.minimum(z, 0.0) - jnp.log1p(jnp.exp(-jnp.abs(z)))


def _gla_block(q_ref, k_ref, v_ref, g, st_ref, o_ref, tri, mask, ref_row, last_row, order):
    n = len(order)
    q_scale = GLA_DK ** -0.5
    rows = {c: slice(c * GLA_CHUNK, (c + 1) * GLA_CHUNK) for c in order}
    b = {}
    for c in order:
        gc = g[rows[c], :]
        g1 = gc.astype(bf16)
        r1 = gc - g1.astype(f32)
        g2 = r1.astype(bf16)
        g3 = (r1 - g2.astype(f32)).astype(bf16)
        b[c] = (jnp.dot(tri, g1, preferred_element_type=f32) + jnp.dot(tri, g2, preferred_element_type=f32)
                + jnp.dot(tri, g3, preferred_element_type=f32))
        yield
    qd, kd, qa, ka, dec = {}, {}, {}, {}, {}
    for c in order:
        q = q_ref[rows[c], :].astype(f32) * q_scale
        k = k_ref[rows[c], :].astype(f32)
        b_ref = b[c][ref_row:ref_row + 1, :]
        b_last = b[c][last_row:last_row + 1, :]
        qd[c] = q * jnp.exp(b[c])
        kd[c] = k * jnp.exp(b_last - b[c])
        qa[c] = (q * jnp.exp(b[c] - b_ref)).astype(bf16)
        ka[c] = (k * jnp.exp(b_ref - b[c])).astype(bf16)
        dec[c] = jnp.exp(b_last)
        yield
    zero = jnp.zeros((GLA_CHUNK, GLA_CHUNK), bf16)
    blocks = [[zero] * n for _ in range(n)]
    for c in order:
        a = lax.dot_general(qa[c], ka[c], NT_DIMS, preferred_element_type=f32)
        blocks[c][c] = jnp.where(mask, a, 0.0).astype(bf16)
        yield
    q_rows = [None] * n
    k_rows = [None] * n
    into = None
    for pi, ci in enumerate(order):
        q_rows[ci] = (qd[ci] if into is None else qd[ci] * into).astype(bf16)
        between = None
        for pj in range(pi - 1, -1, -1):
            cj = order[pj]
            qs = (qd[ci] if between is None else qd[ci] * between).astype(bf16)
            blocks[ci][cj] = lax.dot_general(qs, kd[cj].astype(bf16), NT_DIMS,
                                             preferred_element_type=f32).astype(bf16)
            between = dec[cj] if between is None else between * dec[cj]
        into = dec[ci] if into is None else into * dec[ci]
        yield
    after = None
    for ci in reversed(order):
        k_rows[ci] = (kd[ci] if after is None else kd[ci] * after).astype(bf16)
        after = dec[ci] if after is None else after * dec[ci]
    yield
    a_full = jnp.concatenate([jnp.concatenate(r, axis=1) for r in blocks], axis=0)
    q_blk = jnp.concatenate(q_rows, axis=0)
    v = v_ref[...]
    st = st_ref[...]
    o = (jnp.dot(a_full, v, preferred_element_type=f32)
         + lax.dot_general(q_blk, st.astype(bf16), NT_DIMS, preferred_element_type=f32))
    o_ref[...] = o.astype(o_ref.dtype)
    yield
    k_blk = jnp.concatenate(k_rows, axis=0)
    st_ref[...] = st * into + lax.dot_general(v, k_blk, TN_DIMS, preferred_element_type=f32)


def _gla_kernel(*refs, has_state, emit_state):
    (qf_ref, kf_ref, vf_ref, lf_ref, qb_ref, kb_ref, vb_ref, lb_ref,
     wf_ref, wb_ref, bf_ref, bb_ref) = refs[:12]
    refs = refs[12:]
    if has_state:
        s0f_ref, s0b_ref = refs[:2]
        refs = refs[2:]
    refs = refs[2:]
    of_ref, ob_ref = refs[:2]
    refs = refs[2:]
    if emit_state:
        sf_ref, sb_ref = refs[:2]
        refs = refs[2:]
    stf_ref, stb_ref = refs
    c = pl.program_id(2)
    heads = stf_ref.shape[0]

    @pl.when(c == 0)
    def _():
        for hh in range(heads):
            if has_state:
                stf_ref[hh] = s0f_ref[hh].T
                stb_ref[hh] = s0b_ref[hh].T
            else:
                stf_ref[hh] = jnp.zeros((GLA_DV, GLA_DK), f32)
                stb_ref[hh] = jnp.zeros((GLA_DV, GLA_DK), f32)

    row = lax.broadcasted_iota(jnp.int32, (GLA_CHUNK, GLA_CHUNK), 0)
    col = lax.broadcasted_iota(jnp.int32, (GLA_CHUNK, GLA_CHUNK), 1)
    lower = col <= row
    upper = col >= row
    tri_f = jnp.where(lower, 1.0, 0.0).astype(bf16)
    tri_b = jnp.where(upper, 1.0, 0.0).astype(bf16)
    mid = GLA_CHUNK // 2
    n_chunks = GLA_BLOCK // GLA_CHUNK

    def decay(l_ref, w_ref, b_ref):
        z = jnp.dot(l_ref[...].astype(bf16), w_ref[...], preferred_element_type=f32) + b_ref[...]
        return _log_sigmoid(z) * (1.0 / GK_NORMALIZER)

    g_f = decay(lf_ref, wf_ref, bf_ref)
    g_b = decay(lb_ref, wb_ref, bb_ref)
    order = list(range(n_chunks))
    chains = []
    for hh in range(heads):
        ks = slice(hh * GLA_DK, (hh + 1) * GLA_DK)
        vs = slice(hh * GLA_DV, (hh + 1) * GLA_DV)
        chains.append(_gla_block(qf_ref.at[:, ks], kf_ref.at[:, ks], vf_ref.at[:, vs], g_f[:, ks],
                                 stf_ref.at[hh], of_ref.at[:, vs], tri_f, lower, mid, GLA_CHUNK - 1, order))
        chains.append(_gla_block(qb_ref.at[:, ks], kb_ref.at[:, ks], vb_ref.at[:, vs], g_b[:, ks],
                                 stb_ref.at[hh], ob_ref.at[:, vs], tri_b, upper, GLA_CHUNK - 1 - mid, 0,
                                 order[::-1]))
    while chains:
        chains = [ch for ch in chains if next(ch, True) is None]

    if emit_state:
        @pl.when(c == pl.num_programs(2) - 1)
        def _():
            for hh in range(heads):
                sf_ref[hh] = stf_ref[hh].T
                sb_ref[hh] = stb_ref[hh].T


def _gla(main, lr, of_prev, ob_prev, wgk_f, wgk_b, bgk_f, bgk_b, row0, n_batch, seq,
         states=None, emit_state=False):
    nb = seq // GLA_BLOCK
    assert row0 % GLA_BLOCK == 0
    rb0 = row0 // GLA_BLOCK
    fwd = lambda b, c: rb0 + b * nb + c
    bwd = lambda b, c: rb0 + b * nb + (nb - 1 - c)

    hps = GLA_HEADS_PER_STEP
    dk, dv = hps * GLA_DK, hps * GLA_DV

    def token_specs(pos):
        return [
            pl.BlockSpec((GLA_BLOCK, dk), lambda b, h, c: (pos(b, c), COL_GQ // dk + h)),
            pl.BlockSpec((GLA_BLOCK, dk), lambda b, h, c: (pos(b, c), COL_GK // dk + h)),
            pl.BlockSpec((GLA_BLOCK, dv), lambda b, h, c: (pos(b, c), COL_GV // dv + h)),
            pl.BlockSpec((GLA_BLOCK, LANES), lambda b, h, c: (pos(b, c), 0)),
        ]

    in_specs = token_specs(fwd) + token_specs(bwd) + [
        pl.BlockSpec((LANES, dk), lambda b, h, c: (0, h)),
        pl.BlockSpec((LANES, dk), lambda b, h, c: (0, h)),
        pl.BlockSpec((1, dk), lambda b, h, c: (0, h)),
        pl.BlockSpec((1, dk), lambda b, h, c: (0, h)),
    ]
    args = [main, main, main, lr, main, main, main, lr, wgk_f, wgk_b, bgk_f, bgk_b]
    if states is not None:
        s_f, s_b, layer = states
        spec = pl.BlockSpec((None, None, hps, GLA_DK, GLA_DV), lambda b, h, c: (b, layer, h, 0, 0))
        in_specs += [spec, spec]
        args += [s_f, s_b]
    in_specs += [pl.BlockSpec(memory_space=pl.ANY)] * 2
    alias0 = len(args)
    args += [of_prev, ob_prev]
    out_specs = [
        pl.BlockSpec((GLA_BLOCK, dv), lambda b, h, c: (fwd(b, c), h)),
        pl.BlockSpec((GLA_BLOCK, dv), lambda b, h, c: (bwd(b, c), h)),
    ]
    out_shape = [jax.ShapeDtypeStruct(of_prev.shape, of_prev.dtype),
                 jax.ShapeDtypeStruct(ob_prev.shape, ob_prev.dtype)]
    if emit_state:
        spec = pl.BlockSpec((None, hps, GLA_DK, GLA_DV), lambda b, h, c: (b, h, 0, 0))
        out_specs += [spec, spec]
        out_shape += [jax.ShapeDtypeStruct((n_batch, GLA_HEADS, GLA_DK, GLA_DV), f32)] * 2
    return pl.pallas_call(
        functools.partial(_gla_kernel, has_state=states is not None, emit_state=emit_state),
        grid=(n_batch, GLA_HEADS // hps, nb),
        in_specs=in_specs,
        out_specs=out_specs,
        out_shape=out_shape,
        scratch_shapes=[pltpu.VMEM((hps, GLA_DV, GLA_DK), f32), pltpu.VMEM((hps, GLA_DV, GLA_DK), f32)],
        input_output_aliases={alias0: 0, alias0 + 1: 1},
        compiler_params=_params(("arbitrary", "arbitrary", "arbitrary")),
        name="gla_state" if states is not None else "gla",
    )(*args)


def _merge_kernel(x_ref, mod_ref, oa_ref, of_ref, ob_ref, gg_ref, ga_ref, gb_ref, gn_ref, w_ref, o_ref):
    mix = None
    for hh in range(GLA_HEADS):
        sl = slice(hh * GLA_DV, (hh + 1) * GLA_DV)
        og = of_ref[:, sl].astype(f32) + ob_ref[:, sl].astype(f32)
        gla = _rms(og, gn_ref[...]) * _silu(gg_ref[:, sl].astype(f32))
        merged = (jax.nn.sigmoid(ga_ref[:, sl].astype(f32)) * oa_ref[:, sl].astype(f32)
                  + jax.nn.sigmoid(gb_ref[:, sl].astype(f32)) * gla)
        part = jnp.dot(merged.astype(bf16), w_ref[sl, :], preferred_element_type=f32)
        mix = part if mix is None else mix + part
    o_ref[...] = x_ref[...] + mod_ref[2:3, :] * mix


def _merge_out(x_all, mod, layer, o_att, o_f, o_b, main, gla_norm, w_out, n_prompt, dec_seq):
    t_all = x_all.shape[0]
    tm = _tile(min(n_prompt, dec_seq), 256)
    cidx = _cond_index(tm, n_prompt, dec_seq)
    row = lambda i: (i, 0)
    return pl.pallas_call(
        _merge_kernel,
        grid=(t_all // tm,),
        in_specs=[
            pl.BlockSpec((tm, D_MODEL), row),
            pl.BlockSpec((None, None, 6, D_MODEL), lambda i: (layer, cidx(i), 0, 0)),
            pl.BlockSpec((tm, D_MODEL), row),
            pl.BlockSpec((tm, GLA_VAL), row),
            pl.BlockSpec((tm, GLA_VAL), row),
            pl.BlockSpec((tm, GLA_VAL), lambda i: (i, COL_GG // GLA_VAL)),
            pl.BlockSpec((tm, D_MODEL), lambda i: (i, COL_GA // D_MODEL)),
            pl.BlockSpec((tm, D_MODEL), lambda i: (i, COL_GB // D_MODEL)),
            pl.BlockSpec((1, GLA_DV), lambda i: (0, 0)),
            pl.BlockSpec((D_MODEL, D_MODEL), lambda i: (0, 0)),
        ],
        out_specs=pl.BlockSpec((tm, D_MODEL), row),
        out_shape=jax.ShapeDtypeStruct(x_all.shape, f32),
        compiler_params=_params(("arbitrary",)),
        name="merge_out",
    )(x_all, mod, o_att, o_f, o_b, main, main, main, gla_norm, w_out)


def _ffn_kernel(x_ref, mod_ref, n2_ref, wg_ref, wu_ref, wd_ref, o_ref, h_ref, acc_ref):
    j = pl.program_id(1)

    @pl.when(j == 0)
    def _():
        y = _rms(x_ref[...], n2_ref[...])
        h_ref[...] = (y * (1.0 + mod_ref[4:5, :]) + mod_ref[3:4, :]).astype(bf16)
        acc_ref[...] = jnp.zeros_like(acc_ref)

    h = h_ref[...]
    gate = jnp.dot(h, wg_ref[...], preferred_element_type=f32)
    up = jnp.dot(h, wu_ref[...], preferred_element_type=f32)
    acc_ref[...] += jnp.dot((_silu(gate) * up).astype(bf16), wd_ref[...], preferred_element_type=f32)

    @pl.when(j == pl.num_programs(1) - 1)
    def _():
        o_ref[...] = x_ref[...] + mod_ref[5:6, :] * acc_ref[...]


def _dense_ffn(x_all, mod, layer, norm2, wg, wu, wd, n_prompt, dec_seq):
    t_all = x_all.shape[0]
    tm = _tile(min(n_prompt, dec_seq), 512)
    tf = 512
    cidx = _cond_index(tm, n_prompt, dec_seq)
    return pl.pallas_call(
        _ffn_kernel,
        grid=(t_all // tm, FF_DENSE // tf),
        in_specs=[
            pl.BlockSpec((tm, D_MODEL), lambda i, j: (i, 0)),
            pl.BlockSpec((None, None, 6, D_MODEL), lambda i, j: (layer, cidx(i), 0, 0)),
            pl.BlockSpec((1, D_MODEL), lambda i, j: (0, 0)),
            pl.BlockSpec((D_MODEL, tf), lambda i, j: (0, j)),
            pl.BlockSpec((D_MODEL, tf), lambda i, j: (0, j)),
            pl.BlockSpec((tf, D_MODEL), lambda i, j: (j, 0)),
        ],
        out_specs=pl.BlockSpec((tm, D_MODEL), lambda i, j: (i, 0)),
        out_shape=jax.ShapeDtypeStruct(x_all.shape, f32),
        scratch_shapes=[pltpu.VMEM((tm, D_MODEL), bf16), pltpu.VMEM((tm, D_MODEL), f32)],
        compiler_params=_params(("arbitrary", "arbitrary")),
        name="dense_ffn",
    )(x_all, mod, norm2, wg, wu, wd)


INFO_E1, INFO_E2, INFO_W1, INFO_W2, INFO_R1, INFO_R2 = range(6)


def _route_kernel(x_ref, mod_ref, n2_ref, wr_ref, h_ref, info_ref, cnt_ref, run_ref):
    i = pl.program_id(0)
    tm = x_ref.shape[0]

    @pl.when(i == 0)
    def _():
        run_ref[...] = jnp.zeros_like(run_ref)

    y = _rms(x_ref[...], n2_ref[...])
    h = y * (1.0 + mod_ref[4:5, :]) + mod_ref[3:4, :]
    h_ref[...] = h
    logits = jnp.dot(h, wr_ref[...], preferred_element_type=f32, precision=lax.Precision.HIGHEST)
    lane = lax.broadcasted_iota(jnp.int32, (tm, LANES), 1)
    lane_f = lane.astype(f32)
    neg = -jnp.inf
    lg = jnp.where(lane < N_EXPERTS, logits, neg)
    m1 = jnp.max(lg, axis=-1, keepdims=True)
    i1 = jnp.min(jnp.where(lg == m1, lane_f, float(LANES)), axis=-1, keepdims=True)
    e1 = lane_f == i1
    lg2 = jnp.where(e1, neg, lg)
    m2 = jnp.max(lg2, axis=-1, keepdims=True)
    i2 = jnp.min(jnp.where(lg2 == m2, lane_f, float(LANES)), axis=-1, keepdims=True)
    e2 = lane_f == i2
    ex = jnp.exp(m2 - m1)
    w1 = 1.0 / (1.0 + ex)
    w2 = ex / (1.0 + ex)
    e1f = jnp.where(e1, 1.0, 0.0)
    e2f = jnp.where(e2, 1.0, 0.0)
    both = e1f + e2f
    row = lax.broadcasted_iota(jnp.int32, (tm, tm), 0)
    col = lax.broadcasted_iota(jnp.int32, (tm, tm), 1)
    strict = jnp.where(col < row, 1.0, 0.0).astype(bf16)
    before = jnp.dot(strict, both.astype(bf16), preferred_element_type=f32) + run_ref[...]
    r1 = jnp.sum(before * e1f, axis=-1, keepdims=True)
    r2 = jnp.sum(before * e2f, axis=-1, keepdims=True)
    run = run_ref[...] + jnp.sum(both, axis=0, keepdims=True)
    run_ref[...] = run
    cnt_ref[...] = run
    info = jnp.zeros((tm, LANES), f32)
    for idx, val in ((INFO_E1, i1), (INFO_E2, i2), (INFO_W1, w1), (INFO_W2, w2),
                     (INFO_R1, r1), (INFO_R2, r2)):
        info = jnp.where(lane == idx, val, info)
    info_ref[...] = info


def _route(x_all, mod, layer, norm2, w_router, n_prompt, dec_seq):
    t_all = x_all.shape[0]
    tm = _tile(min(n_prompt, dec_seq), 512)
    cidx = _cond_index(tm, n_prompt, dec_seq)
    return pl.pallas_call(
        _route_kernel,
        grid=(t_all // tm,),
        in_specs=[
            pl.BlockSpec((tm, D_MODEL), lambda i: (i, 0)),
            pl.BlockSpec((None, None, 6, D_MODEL), lambda i: (layer, cidx(i), 0, 0)),
            pl.BlockSpec((1, D_MODEL), lambda i: (0, 0)),
            pl.BlockSpec((D_MODEL, LANES), lambda i: (0, 0)),
        ],
        out_specs=[
            pl.BlockSpec((tm, D_MODEL), lambda i: (i, 0)),
            pl.BlockSpec((tm, LANES), lambda i: (i, 0)),
            pl.BlockSpec((1, LANES), lambda i: (0, 0)),
        ],
        out_shape=[
            jax.ShapeDtypeStruct((t_all, D_MODEL), f32),
            jax.ShapeDtypeStruct((t_all, LANES), f32),
            jax.ShapeDtypeStruct((1, LANES), f32),
        ],
        scratch_shapes=[pltpu.VMEM((1, LANES), f32)],
        compiler_params=_params(("arbitrary",)),
        name="route",
    )(x_all, mod, norm2, w_router)


def _row_copy(src, src_row, dst, dst_row, sem):
    return pltpu.make_async_copy(src.at[pl.ds(src_row, 1)], dst.at[pl.ds(dst_row, 1)], sem)


def _dispatch_kernel(tok_ref, h_hbm, o_ref, buf_ref, sem, *, rows):
    i = pl.program_id(0)

    def gather(block):
        slot = block % 2

        def issue(r, carry):
            _row_copy(h_hbm, tok_ref[block * rows + r], buf_ref.at[slot], r, sem.at[slot]).start()
            return carry

        lax.fori_loop(0, rows, issue, 0, unroll=DMA_UNROLL)

    pl.when(i == 0)(lambda: gather(i))
    pl.when(i + 1 < pl.num_programs(0))(lambda: gather(i + 1))
    slot = i % 2

    def drain(r, carry):
        _row_copy(h_hbm, 0, buf_ref.at[slot], 0, sem.at[slot]).wait()
        return carry

    lax.fori_loop(0, rows, drain, 0, unroll=DMA_UNROLL)
    o_ref[...] = buf_ref[slot].astype(bf16)


def _dispatch(slot_token, h_all):
    n_rows = slot_token.shape[0]
    rows = _tile(n_rows, 1024)
    return pl.pallas_call(
        functools.partial(_dispatch_kernel, rows=rows),
        grid_spec=pltpu.PrefetchScalarGridSpec(
            num_scalar_prefetch=1,
            grid=(n_rows // rows,),
            in_specs=[pl.BlockSpec(memory_space=pl.ANY)],
            out_specs=pl.BlockSpec((rows, D_MODEL), lambda i, t: (i, 0)),
            scratch_shapes=[pltpu.VMEM((2, rows, D_MODEL), f32), pltpu.SemaphoreType.DMA((2,))],
        ),
        out_shape=jax.ShapeDtypeStruct((n_rows, D_MODEL), bf16),
        compiler_params=_params(("arbitrary",)),
        name="moe_dispatch",
    )(slot_token, h_all)


def _expert_kernel(be_ref, na_ref, x_ref, wg_ref, wu_ref, wd_ref, o_ref, acc_ref):
    b = pl.program_id(0)
    j = pl.program_id(1)
    active = b < na_ref[0]

    @pl.when(active & (j == 0))
    def _():
        acc_ref[...] = jnp.zeros_like(acc_ref)

    @pl.when(active)
    def _():
        h = x_ref[...]
        gate = jnp.dot(h, wg_ref[...], preferred_element_type=f32)
        up = jnp.dot(h, wu_ref[...], preferred_element_type=f32)
        acc_ref[...] += jnp.dot((_silu(gate) * up).astype(bf16), wd_ref[...], preferred_element_type=f32)

    @pl.when(j == pl.num_programs(1) - 1)
    def _():
        o_ref[...] = jnp.where(active, acc_ref[...], 0.0)


def _expert_ffn(block_expert, n_active, sorted_h, wg, wu, wd, tmb):
    n_rows = sorted_h.shape[0]
    tf = 1024
    nf = FF_EXPERT // tf

    def ff(b, j, na):
        return jnp.where(b < na[0], j, nf - 1)

    return pl.pallas_call(
        _expert_kernel,
        grid_spec=pltpu.PrefetchScalarGridSpec(
            num_scalar_prefetch=2,
            grid=(n_rows // tmb, nf),
            in_specs=[
                pl.BlockSpec((tmb, D_MODEL), lambda b, j, be, na: (b, 0)),
                pl.BlockSpec((None, D_MODEL, tf), lambda b, j, be, na: (be[b], 0, ff(b, j, na))),
                pl.BlockSpec((None, D_MODEL, tf), lambda b, j, be, na: (be[b], 0, ff(b, j, na))),
                pl.BlockSpec((None, tf, D_MODEL), lambda b, j, be, na: (be[b], ff(b, j, na), 0)),
            ],
            out_specs=pl.BlockSpec((tmb, D_MODEL), lambda b, j, be, na: (b, 0)),
            scratch_shapes=[pltpu.VMEM((tmb, D_MODEL), f32)],
        ),
        out_shape=jax.ShapeDtypeStruct((n_rows, D_MODEL), f32),
        compiler_params=_params(("arbitrary", "arbitrary")),
        name="expert_ffn",
    )(block_expert, n_active, sorted_h, wg, wu, wd)


def _combine_kernel(dest_ref, x_ref, mod_ref, info_ref, y_hbm, o_ref, buf_ref, sem, *, rows):
    i = pl.program_id(0)

    def gather(block):
        slot = block % 2

        def issue(r, carry):
            for kk in range(2):
                _row_copy(y_hbm, dest_ref[2 * (block * rows + r) + kk], buf_ref.at[slot, kk], r,
                          sem.at[slot]).start()
            return carry

        lax.fori_loop(0, rows, issue, 0, unroll=DMA_UNROLL)

    pl.when(i == 0)(lambda: gather(i))
    pl.when(i + 1 < pl.num_programs(0))(lambda: gather(i + 1))
    slot = i % 2

    def drain(r, carry):
        for kk in range(2):
            _row_copy(y_hbm, 0, buf_ref.at[slot, kk], 0, sem.at[slot]).wait()
        return carry

    lax.fori_loop(0, rows, drain, 0, unroll=DMA_UNROLL)
    info = info_ref[...]
    w1 = info[:, INFO_W1:INFO_W1 + 1]
    w2 = info[:, INFO_W2:INFO_W2 + 1]
    o_ref[...] = x_ref[...] + mod_ref[5:6, :] * (w1 * buf_ref[slot, 0] + w2 * buf_ref[slot, 1])


def _combine(dest, x_all, mod, layer, info, y_sorted, n_prompt, dec_seq):
    t_all = x_all.shape[0]
    rows = _tile(min(n_prompt, dec_seq), 512)
    cidx = _cond_index(rows, n_prompt, dec_seq)
    return pl.pallas_call(
        functools.partial(_combine_kernel, rows=rows),
        grid_spec=pltpu.PrefetchScalarGridSpec(
            num_scalar_prefetch=1,
            grid=(t_all // rows,),
            in_specs=[
                pl.BlockSpec((rows, D_MODEL), lambda i, d: (i, 0)),
                pl.BlockSpec((None, None, 6, D_MODEL), lambda i, d: (layer, cidx(i), 0, 0)),
                pl.BlockSpec((rows, LANES), lambda i, d: (i, 0)),
                pl.BlockSpec(memory_space=pl.ANY),
            ],
            out_specs=pl.BlockSpec((rows, D_MODEL), lambda i, d: (i, 0)),
            scratch_shapes=[pltpu.VMEM((2, 2, rows, D_MODEL), f32), pltpu.SemaphoreType.DMA((2,))],
        ),
        out_shape=jax.ShapeDtypeStruct(x_all.shape, f32),
        compiler_params=_params(("arbitrary",)),
        name="moe_combine",
    )(dest, x_all, mod, info, y_sorted)


def _moe_ffn(x_all, mod, layer, norm2, w_router, wg, wu, wd, n_prompt, dec_seq):
    t_all = x_all.shape[0]
    tmb = _tile(t_all, 512)
    n_blocks = (2 * t_all) // tmb + N_EXPERTS
    h_all, info, counts = _route(x_all, mod, layer, norm2, w_router, n_prompt, dec_seq)
    counts = counts[0, :N_EXPERTS].astype(jnp.int32)
    padded = ((counts + tmb - 1) // tmb) * tmb
    end = jnp.cumsum(padded)
    start = end - padded
    e12 = info[:, INFO_E1:INFO_E2 + 1].astype(jnp.int32)
    r12 = info[:, INFO_R1:INFO_R2 + 1].astype(jnp.int32)
    dest = (start[e12] + r12).reshape(2 * t_all)
    slot_token = jnp.zeros((n_blocks * tmb,), jnp.int32).at[dest].set(
        jnp.arange(2 * t_all, dtype=jnp.int32) // 2, unique_indices=True)
    n_active = (end[-1] // tmb).astype(jnp.int32).reshape(1)
    blk = jnp.minimum(jnp.arange(n_blocks, dtype=jnp.int32), n_active[0] - 1) * tmb
    block_expert = jnp.minimum(jnp.sum(blk[:, None] >= end[None, :], axis=1), N_EXPERTS - 1).astype(jnp.int32)
    sorted_h = _dispatch(slot_token, h_all)
    y_sorted = _expert_ffn(block_expert, n_active, sorted_h, wg, wu, wd, tmb)
    return _combine(dest, x_all, mod, layer, info, y_sorted, n_prompt, dec_seq)


def _final_kernel(x_ref, g_ref, o_ref):
    o_ref[...] = _rms(x_ref[...], g_ref[...])


def _final_norm(x_all, gain, row0, n_rows):
    tm = _tile(n_rows, 512)
    assert row0 % tm == 0
    return pl.pallas_call(
        _final_kernel,
        grid=(n_rows // tm,),
        in_specs=[pl.BlockSpec((tm, D_MODEL), lambda i: (row0 // tm + i, 0)),
                  pl.BlockSpec((1, D_MODEL), lambda i: (0, 0))],
        out_specs=pl.BlockSpec((tm, D_MODEL), lambda i: (i, 0)),
        out_shape=jax.ShapeDtypeStruct((n_rows, D_MODEL), f32),
        compiler_params=_params(("arbitrary",)),
        name="final_norm",
    )(x_all, gain)


def _rope_tables(dec_seq, tile_rows):
    rows = dec_seq // GRID_W
    row = jnp.repeat(jnp.arange(rows, dtype=f32), GRID_W)
    col = jnp.tile(jnp.arange(GRID_W, dtype=f32), rows)
    freqs = ROPE_THETA ** (-jnp.arange(AXIS_PAIRS, dtype=f32) / AXIS_PAIRS)
    ang = jnp.concatenate([row[:, None] * freqs, col[:, None] * freqs], axis=-1)
    cos, sin = jnp.cos(ang), jnp.sin(ang)
    cos_t = jnp.concatenate([cos, cos], axis=-1)
    sin_t = jnp.concatenate([-sin, sin], axis=-1)
    cos_t = jnp.concatenate([cos_t, jnp.ones((tile_rows, HEAD_DIM), f32)], axis=0)
    sin_t = jnp.concatenate([sin_t, jnp.zeros((tile_rows, HEAD_DIM), f32)], axis=0)
    return cos_t, sin_t


def _split_w_in(w):
    sizes = (ATT_HEADS * HEAD_DIM, KV_HEADS * HEAD_DIM, KV_HEADS * HEAD_DIM, GLA_KEY, GLA_KEY, GLA_VAL,
             GLA_VAL, GK_RANK, GK_RANK, D_MODEL, D_MODEL)
    offs = np.concatenate([[0], np.cumsum(sizes)])
    q, k, v, gq, gk, gv, gg, lrf, lrb, ga, gb = (w[:, offs[n]:offs[n + 1]] for n in range(len(sizes)))
    main = jnp.concatenate([q, gv, gg, ga, gb, gq, gk, k, v], axis=1).astype(bf16)
    lr = jnp.concatenate([lrf, lrb, jnp.zeros((D_MODEL, LANES - 2 * GK_RANK), w.dtype)], axis=1).astype(bf16)
    return main, lr


def kernel(x_prompt, x_sample, cache_k, cache_v, state_fwd, state_bwd, c, c_ctx, w_mod, b_mod, norm1, w_in,
           q_norm, k_norm, w_gk, b_gk, gla_norm, w_out, norm2, w_ff_gate, w_ff_up, w_ff_down, w_router,
           w_e_gate, w_e_up, w_e_down, final_norm):
    batch, seq, _ = x_prompt.shape
    dec_batch, dec_seq, _ = x_sample.shape
    depth = w_mod.shape[0]
    past = cache_k.shape[2]
    n_prompt = batch * seq
    n_sample = dec_batch * dec_seq
    t_all = n_prompt + n_sample
    assert dec_batch + 1 <= N_COND_PAD and n_prompt % dec_seq == 0 and seq % GLA_BLOCK == 0

    x_all = jnp.concatenate([x_prompt.reshape(n_prompt, D_MODEL), x_sample.reshape(n_sample, D_MODEL)], axis=0)
    cond = jnp.concatenate([c_ctx[None, :], c, jnp.zeros((N_COND_PAD - 1 - dec_batch, D_MODEL), f32)], axis=0)
    mod = _modulation(cond, w_mod, b_mod)

    proj_tm = _tile(min(n_prompt, dec_seq), 1024)
    cos_t, sin_t = _rope_tables(dec_seq, proj_tm)
    ck = cache_k.reshape(dec_batch, depth, past, KV_HEADS * HEAD_DIM)
    cv = cache_v.reshape(dec_batch, depth, past, KV_HEADS * HEAD_DIM)

    o_att = jnp.zeros((t_all, D_MODEL), bf16)
    o_f = jnp.zeros((t_all, GLA_VAL), bf16)
    o_b = jnp.zeros((t_all, GLA_VAL), bf16)
    new_k, new_v, new_sf, new_sb = [], [], [], []
    for l in range(depth):
        w_main, w_lr = _split_w_in(w_in[l])
        main, lr, kv32 = _in_proj(x_all, mod, l, norm1[l][None, :], w_main, w_lr, q_norm[l][None, :],
                                  k_norm[l][None, :], cos_t, sin_t, n_prompt, dec_seq)
        new_k.append(kv32[:n_prompt, :PROJ_TN].reshape(batch, seq, KV_HEADS, HEAD_DIM))
        new_v.append(kv32[:n_prompt, PROJ_TN:].reshape(batch, seq, KV_HEADS, HEAD_DIM))

        o_att = _attention(main, o_att, 0, batch, seq)
        o_att = _attention(main, o_att, n_prompt, dec_batch, dec_seq, ctx=(ck, cv, l))

        pad = jnp.zeros((LANES - GK_RANK, GLA_KEY), f32)
        wgk_f = jnp.concatenate([w_gk[l, 0], pad], axis=0).astype(bf16)
        wgk_b = jnp.concatenate([pad[:GK_RANK], w_gk[l, 1], pad[:LANES - 2 * GK_RANK]], axis=0).astype(bf16)
        bgk_f, bgk_b = b_gk[l, 0][None, :], b_gk[l, 1][None, :]
        o_f, o_b, s_f, s_b = _gla(main, lr, o_f, o_b, wgk_f, wgk_b, bgk_f, bgk_b, 0, batch, seq,
                                  emit_state=True)
        new_sf.append(s_f)
        new_sb.append(s_b)
        o_f, o_b = _gla(main, lr, o_f, o_b, wgk_f, wgk_b, bgk_f, bgk_b, n_prompt, dec_batch, dec_seq,
                        states=(state_fwd, state_bwd, l))

        x_all = _merge_out(x_all, mod, l, o_att, o_f, o_b, main, gla_norm[l][None, :], w_out[l].astype(bf16),
                           n_prompt, dec_seq)
        i = l // 2
        if l % 2 == 1:
            w_r = jnp.concatenate([w_router[i], jnp.zeros((D_MODEL, LANES - N_EXPERTS), f32)], axis=1)
            x_all = _moe_ffn(x_all, mod, l, norm2[l][None, :], w_r, w_e_gate[i].astype(bf16),
                             w_e_up[i].astype(bf16), w_e_down[i].astype(bf16), n_prompt, dec_seq)
        else:
            x_all = _dense_ffn(x_all, mod, l, norm2[l][None, :], w_ff_gate[i].astype(bf16),
                               w_ff_up[i].astype(bf16), w_ff_down[i].astype(bf16), n_prompt, dec_seq)

    y_prompt = _final_norm(x_all, final_norm[None, :], 0, n_prompt).reshape(batch, seq, D_MODEL)
    y_sample = _final_norm(x_all, final_norm[None, :], n_prompt, n_sample).reshape(dec_batch, dec_seq, D_MODEL)
    return (y_prompt, y_sample, jnp.stack(new_k, axis=1), jnp.stack(new_v, axis=1),
            jnp.stack(new_sf, axis=1), jnp.stack(new_sb, axis=1))
```
